```python
import jax, jax.numpy as jnp
from jax import lax
import numpy as np

D_MODEL = 4096
BATCH = 2
SEQ = 8192
DEPTH = 2

HEAD_DIM = 128
N_HEADS = D_MODEL // 256
W_ATTN = N_HEADS * HEAD_DIM
SGU_DIM = 128
SGU_GROUPS = D_MODEL // 256
W_SGU = SGU_GROUPS * SGU_DIM
CHUNK = 128
Q_BLOCK = 128
D_FF = 4 * D_MODEL
N_IN = 3 * W_ATTN + N_HEADS + 2 * W_SGU + 2 * D_MODEL
N_MOD = 6 * D_MODEL
EPS = 1e-6

kernel_name = "fox_sgu_parallel_hybrid_block"


def _rmsnorm(x, g):
    xf = x.astype(jnp.float32)
    y = xf * lax.rsqrt(jnp.mean(xf * xf, axis=-1, keepdims=True) + EPS)
    return (y * g.astype(jnp.float32)).astype(x.dtype)


def _forgetting_attention(q, k, v, log_f):
    B, S, H, Dh = q.shape
    nb = S // Q_BLOCK
    F = jnp.transpose(lax.cumsum(log_f, axis=1), (0, 2, 1))
    qb = q.reshape(B, nb, Q_BLOCK, H, Dh).transpose(1, 0, 2, 3, 4)
    Fq = F.reshape(B, H, nb, Q_BLOCK).transpose(2, 0, 1, 3)
    k_pos = jnp.arange(S)
    scale = HEAD_DIM ** -0.5

    def block(args):
        qi, Fqi, i = args
        s = jnp.einsum('bqhd,bkhd->bhqk', qi, k, preferred_element_type=jnp.float32) * scale
        s = s + Fqi[..., None] - F[:, :, None, :]
        q_pos = i * Q_BLOCK + jnp.arange(Q_BLOCK)
        s = jnp.where(k_pos[None, :] <= q_pos[:, None], s, -jnp.inf)
        p = jax.nn.softmax(s, axis=-1)
        return jnp.einsum('bhqk,bkhd->bqhd', p.astype(v.dtype), v)

    out = lax.map(block, (qb, Fq, jnp.arange(nb)))
    return out.transpose(1, 0, 2, 3, 4).reshape(B, S, H * Dh)


def _chunked_sgu(u, v, g_v, w_s, b_s):
    B, S, W = u.shape
    nc = S // CHUNK
    v = _rmsnorm(v, g_v).reshape(B, nc, CHUNK, SGU_GROUPS, SGU_DIM)
    causal = jnp.tril(jnp.ones((CHUNK, CHUNK), dtype=bool))
    w = jnp.where(causal, w_s, 0).astype(v.dtype)
    mixed = jnp.einsum('gts,bnsgc->bntgc', w, v) + b_s.T[:, :, None].astype(v.dtype)
    return u * mixed.reshape(B, S, W)


def setup_inputs(seed: int = 0) -> dict:
    key = jax.random.key(seed)
    ks = jax.random.split(key, 20)
    L, D = DEPTH, D_MODEL
    nrm = lambda k, shape, s: jax.random.normal(k, shape, jnp.float32) * s
    b_f = (jnp.linspace(1.0, 6.0, N_HEADS, dtype=jnp.float32)[None, :]
           + nrm(ks[6], (L, N_HEADS), 0.1))
    return {
        "x": nrm(ks[0], (BATCH, SEQ, D), 1.0),
        "c": nrm(ks[1], (BATCH, D), 1.0),
        "w_mod": nrm(ks[2], (L, D, N_MOD), 0.5 * D ** -0.5),
        "b_mod": nrm(ks[3], (L, N_MOD), 0.01),
        "g_mix": 1.0 + nrm(ks[4], (L, D), 0.02),
        "w_in": nrm(ks[5], (L, D, N_IN), D ** -0.5),
        "b_f": b_f,
        "g_v": 1.0 + nrm(ks[7], (L, W_SGU), 0.02),
        "w_s": nrm(ks[8], (L, SGU_GROUPS, CHUNK, CHUNK), CHUNK ** -0.5),
        "b_s": 1.0 + nrm(ks[9], (L, SGU_GROUPS, CHUNK), 0.01),
        "w_pa": nrm(ks[10], (L, W_ATTN, D), W_ATTN ** -0.5),
        "w_pm": nrm(ks[11], (L, W_SGU, D), W_SGU ** -0.5),
        "w_o": nrm(ks[12], (L, D, D), D ** -0.5),
        "g_ffn": 1.0 + nrm(ks[13], (L, D), 0.02),
        "w_up": nrm(ks[14], (L, D, D_FF), D ** -0.5),
        "w_down": nrm(ks[15], (L, D_FF, D), D_FF ** -0.5),
        "g_final": 1.0 + nrm(ks[16], (D,), 0.02),
    }


def reference(x, c, w_mod, b_mod, g_mix, w_in, b_f, g_v, w_s, b_s,
              w_pa, w_pm, w_o, g_ffn, w_up, w_down, g_final):
    B, S, D = x.shape
    cond = jax.nn.silu(c)
    splits = np.cumsum([W_ATTN, W_ATTN, W_ATTN, N_HEADS, W_SGU, W_SGU, D_MODEL])[:-0 or None].tolist()
    for l in range(DEPTH):
        mod = cond @ w_mod[l] + b_mod[l]
        sh1, sc1, gt1, sh2, sc2, gt2 = jnp.split(mod[:, None, :], 6, axis=-1)

        h = _rmsnorm(x, g_mix[l]) * (1.0 + sc1) + sh1
        z = h @ w_in[l]
        zq, zk, zv, zf, zu, zg, zga, zgm = jnp.split(z, splits[:7], axis=-1)
        q = zq.reshape(B, S, N_HEADS, HEAD_DIM)
        k = zk.reshape(B, S, N_HEADS, HEAD_DIM)
        v = zv.reshape(B, S, N_HEADS, HEAD_DIM)
        log_f = jax.nn.log_sigmoid(zf.astype(jnp.float32) + b_f[l].astype(jnp.float32))
        a = _forgetting_attention(q, k, v, log_f)
        m = _chunked_sgu(jax.nn.gelu(zu), jax.nn.gelu(zg), g_v[l], w_s[l], b_s[l])
        y = jax.nn.sigmoid(zga) * (a @ w_pa[l]) + jax.nn.sigmoid(zgm) * (m @ w_pm[l])
        x = x + gt1 * (y @ w_o[l])

        h2 = _rmsnorm(x, g_ffn[l]) * (1.0 + sc2) + sh2
        x = x + gt2 * (jnp.square(jax.nn.relu(h2 @ w_up[l])) @ w_down[l])
    return _rmsnorm(x, g_final)
```

```python
import functools
import math

import jax
import jax.numpy as jnp
import numpy as np
from jax import lax
from jax.experimental import pallas as pl
from jax.experimental.pallas import tpu as pltpu

HEAD_DIM = 128
SGU_DIM = 128
CHUNK = 128
EPS = 1e-6
LANES = 128
SUBLANES = 8

F32 = jnp.float32
BF16 = jnp.bfloat16


def _pick_block(dim, preferred, align=LANES):
    best = None
    for cand in range(align, min(dim, preferred) + 1, align):
        if dim % cand == 0:
            best = cand
    assert best is not None, (dim, preferred, align)
    return best


def _params(dims, vmem_mb):
    return pltpu.CompilerParams(dimension_semantics=dims,
                                vmem_limit_bytes=vmem_mb * 1024 * 1024)


def _mod_kernel(c_ref, w_ref, b_ref, o_ref):
    c = c_ref[...]
    cond = (c * jax.nn.sigmoid(c)).astype(BF16)
    w = w_ref[0].astype(BF16)
    o_ref[0] = jnp.dot(cond, w, preferred_element_type=F32) + b_ref[0]


def _modulation(c, w_mod, b_mod):
    L, D, NM = w_mod.shape
    B = c.shape[0]
    rows = SUBLANES
    c_pad = jnp.zeros((rows, D), F32).at[:B].set(c)
    bn = _pick_block(NM, 512)
    out = pl.pallas_call(
        _mod_kernel,
        grid=(L, NM // bn),
        in_specs=[
            pl.BlockSpec((rows, D), lambda l, j: (0, 0)),
            pl.BlockSpec((1, D, bn), lambda l, j: (l, 0, j)),
            pl.BlockSpec((1, 1, bn), lambda l, j: (l, 0, j)),
        ],
        out_specs=pl.BlockSpec((1, rows, bn), lambda l, j: (l, 0, j)),
        out_shape=jax.ShapeDtypeStruct((L, rows, NM), F32),
        compiler_params=_params(("parallel", "parallel"), 40),
        name="modulation",
    )(c_pad, w_mod, b_mod.reshape(L, 1, NM))
    return out[:, :B]


def _norm_rows(x, g):
    ms = jnp.mean(x * x, axis=-1, keepdims=True)
    return x * lax.rsqrt(ms + EPS) * g


def _prenorm_kernel(x_ref, g_ref, sc_ref, sh_ref, o_ref):
    y = _norm_rows(x_ref[...], g_ref[...])
    o_ref[...] = (y * (1.0 + sc_ref[0]) + sh_ref[0]).astype(o_ref.dtype)


def _rmsnorm_kernel(x_ref, g_ref, o_ref):
    o_ref[...] = _norm_rows(x_ref[...], g_ref[...]).astype(o_ref.dtype)


def _log_sigmoid(x):
    return jnp.minimum(x, 0.0) - jnp.log1p(jnp.exp(-jnp.abs(x)))


def _prenorm_forget_kernel(x_ref, g_ref, sc_ref, sh_ref, wf_ref, bf_ref,
                           h_ref, f_ref, carry_ref, *, blocks_per_seq):
    i = pl.program_id(0)
    y = _norm_rows(x_ref[...], g_ref[...])
    h = (y * (1.0 + sc_ref[0]) + sh_ref[0]).astype(BF16)
    h_ref[...] = h
    zf = jnp.dot(h, wf_ref[...], preferred_element_type=F32) + bf_ref[...]
    run = _log_sigmoid(zf)
    bm = run.shape[0]
    row = lax.broadcasted_iota(jnp.int32, run.shape, 0)
    shift = 1
    while shift < bm:
        run = run + jnp.where(row >= shift, pltpu.roll(run, shift, axis=0), 0.0)
        shift *= 2

    @pl.when(i % blocks_per_seq == 0)
    def _():
        carry_ref[...] = jnp.zeros_like(carry_ref)

    run = run + carry_ref[0:1, :]
    f_ref[...] = run
    carry_ref[...] = jnp.broadcast_to(run[bm - 1:bm, :], carry_ref.shape)


def _prenorm(x, g, sc, sh, *, bm=256):
    T, D = x.shape
    B = sc.shape[0]
    S = T // B
    return pl.pallas_call(
        _prenorm_kernel,
        grid=(T // bm,),
        in_specs=[
            pl.BlockSpec((bm, D), lambda i: (i, 0)),
            pl.BlockSpec((1, D), lambda i: (0, 0)),
            pl.BlockSpec((1, 1, D), lambda i: (i // (S // bm), 0, 0)),
            pl.BlockSpec((1, 1, D), lambda i: (i // (S // bm), 0, 0)),
        ],
        out_specs=pl.BlockSpec((bm, D), lambda i: (i, 0)),
        out_shape=jax.ShapeDtypeStruct((T, D), BF16),
        compiler_params=_params(("parallel",), 32),
        name="prenorm",
    )(x, g.reshape(1, D), sc, sh)


def _prenorm_forget(x, g, sc, sh, w_f, b_f, *, bm=256):
    T, D = x.shape
    B = sc.shape[0]
    S = T // B
    kern = functools.partial(_prenorm_forget_kernel, blocks_per_seq=S // bm)
    return pl.pallas_call(
        kern,
        grid=(T // bm,),
        in_specs=[
            pl.BlockSpec((bm, D), lambda i: (i, 0)),
            pl.BlockSpec((1, D), lambda i: (0, 0)),
            pl.BlockSpec((1, 1, D), lambda i: (i // (S // bm), 0, 0)),
            pl.BlockSpec((1, 1, D), lambda i: (i // (S // bm), 0, 0)),
            pl.BlockSpec((D, LANES), lambda i: (0, 0)),
            pl.BlockSpec((1, LANES), lambda i: (0, 0)),
        ],
        out_specs=[
            pl.BlockSpec((bm, D), lambda i: (i, 0)),
            pl.BlockSpec((bm, LANES), lambda i: (i, 0)),
        ],
        out_shape=[
            jax.ShapeDtypeStruct((T, D), BF16),
            jax.ShapeDtypeStruct((T, LANES), F32),
        ],
        scratch_shapes=[pltpu.VMEM((SUBLANES, LANES), F32)],
        compiler_params=_params(("arbitrary",), 32),
        name="prenorm_forget",
    )(x, g.reshape(1, D), sc, sh, w_f, b_f)


def _rmsnorm(x, g, *, bm=256):
    T, D = x.shape
    return pl.pallas_call(
        _rmsnorm_kernel,
        grid=(T // bm,),
        in_specs=[
            pl.BlockSpec((bm, D), lambda i: (i, 0)),
            pl.BlockSpec((1, D), lambda i: (0, 0)),
        ],
        out_specs=pl.BlockSpec((bm, D), lambda i: (i, 0)),
        out_shape=jax.ShapeDtypeStruct((T, D), x.dtype),
        compiler_params=_params(("parallel",), 32),
        name="final_rmsnorm",
    )(x, g.reshape(1, D))


def _gelu_tanh(x):
    c = math.sqrt(2.0 / math.pi)
    return 0.5 * x * (1.0 + jnp.tanh(c * (x + 0.044715 * (x * x * x))))


def _apply_act(acc, act):
    if act == "gelu":
        return _gelu_tanh(acc)
    if act == "relu2":
        return jnp.square(jnp.maximum(acc, 0.0))
    if act == "sigmoid":
        return jax.nn.sigmoid(acc)
    return acc


def _mm_act_kernel(x_ref, w_ref, o_ref, *, act):
    acc = jnp.dot(x_ref[...], w_ref[...], preferred_element_type=F32)
    o_ref[...] = _apply_act(acc, act).astype(o_ref.dtype)


def _matmul_act(x, w, act, *, bm=1024, bn=1024, name):
    M, K = x.shape
    N = w.shape[1]
    bm = _pick_block(M, bm)
    bn = _pick_block(N, bn)
    return pl.pallas_call(
        functools.partial(_mm_act_kernel, act=act),
        grid=(M // bm, N // bn),
        in_specs=[
            pl.BlockSpec((bm, K), lambda i, j: (i, 0)),
            pl.BlockSpec((K, bn), lambda i, j: (0, j)),
        ],
        out_specs=pl.BlockSpec((bm, bn), lambda i, j: (i, j)),
        out_shape=jax.ShapeDtypeStruct((M, N), BF16),
        compiler_params=_params(("parallel", "parallel"), 56),
        name=name,
    )(x, w)


def _mm_res_kernel(x_ref, w_ref, r_ref, gt_ref, o_ref, acc_ref):
    k = pl.program_id(2)
    part = jnp.dot(x_ref[...], w_ref[...], preferred_element_type=F32)

    @pl.when(k == 0)
    def _():
        acc_ref[...] = part

    @pl.when(k > 0)
    def _():
        acc_ref[...] += part

    @pl.when(k == pl.num_programs(2) - 1)
    def _():
        o_ref[...] = r_ref[...] + gt_ref[0] * acc_ref[...]


def _matmul_residual(x, w, res, gate, *, bm=1024, bn=1024, bk=2048, name):
    M, K = x.shape
    N = w.shape[1]
    B = gate.shape[0]
    S = M // B
    bm = _pick_block(S, bm)
    bn = _pick_block(N, bn)
    bk = _pick_block(K, bk)
    return pl.pallas_call(
        _mm_res_kernel,
        grid=(M // bm, N // bn, K // bk),
        in_specs=[
            pl.BlockSpec((bm, bk), lambda i, j, k: (i, k)),
            pl.BlockSpec((bk, bn), lambda i, j, k: (k, j)),
            pl.BlockSpec((bm, bn), lambda i, j, k: (i, j)),
            pl.BlockSpec((1, 1, bn), lambda i, j, k: (i // (S // bm), 0, j)),
        ],
        out_specs=pl.BlockSpec((bm, bn), lambda i, j, k: (i, j)),
        out_shape=jax.ShapeDtypeStruct((M, N), F32),
        scratch_shapes=[pltpu.VMEM((bm, bn), F32)],
        compiler_params=_params(("parallel", "parallel", "arbitrary"), 56),
        name=name,
    )(x, w, res, gate)


def _merge_kernel(a_ref, m_ref, wa_ref, wm_ref, ga_ref, gm_ref, o_ref):
    pa = jnp.dot(a_ref[...], wa_ref[...], preferred_element_type=F32)
    pm = jnp.dot(m_ref[...], wm_ref[...], preferred_element_type=F32)
    y = ga_ref[...].astype(F32) * pa + gm_ref[...].astype(F32) * pm
    o_ref[...] = y.astype(o_ref.dtype)


def _branch_merge(a, m, w_pa, w_pm, gates, *, bm=1024, bn=512):
    M, Ka = a.shape
    Km = m.shape[1]
    N = w_pa.shape[1]
    bm = _pick_block(M, bm)
    bn = _pick_block(N, bn)
    nb = N // bn
    return pl.pallas_call(
        _merge_kernel,
        grid=(M // bm, nb),
        in_specs=[
            pl.BlockSpec((bm, Ka), lambda i, j: (i, 0)),
            pl.BlockSpec((bm, Km), lambda i, j: (i, 0)),
            pl.BlockSpec((Ka, bn), lambda i, j: (0, j)),
            pl.BlockSpec((Km, bn), lambda i, j: (0, j)),
            pl.BlockSpec((bm, bn), lambda i, j: (i, j)),
            pl.BlockSpec((bm, bn), lambda i, j: (i, j + nb)),
        ],
        out_specs=pl.BlockSpec((bm, bn), lambda i, j: (i, j)),
        out_shape=jax.ShapeDtypeStruct((M, N), BF16),
        compiler_params=_params(("parallel", "parallel"), 56),
        name="branch_merge",
    )(a, m, w_pa, w_pm, gates, gates)


def _attn_kernel(qi_tbl, ki_tbl, q_ref, k_ref, v_ref, fq_ref, fk_ref, o_ref,
                 m_sc, l_sc, acc_sc, fq_sc, *, scale):
    h = pl.program_id(1)
    p_idx = pl.program_id(2)
    qi = qi_tbl[p_idx]
    ki = ki_tbl[p_idx]

    @pl.when(ki == 0)
    def _():
        m_sc[...] = jnp.full_like(m_sc, -jnp.inf)
        l_sc[...] = jnp.zeros_like(l_sc)
        acc_sc[...] = jnp.zeros_like(acc_sc)
        f_all = fq_ref[...]
        lane = lax.broadcasted_iota(jnp.int32, f_all.shape, 1)
        fq_sc[...] = jnp.sum(jnp.where(lane == h, f_all, 0.0), axis=1, keepdims=True)

    def step(masked):
        s = lax.dot_general(q_ref[...], k_ref[...], (((1,), (1,)), ((), ())),
                            preferred_element_type=F32)
        s = s * scale + fq_sc[...] - fk_ref[0, 0]
        if masked:
            row = lax.broadcasted_iota(jnp.int32, s.shape, 0)
            col = lax.broadcasted_iota(jnp.int32, s.shape, 1)
            s = jnp.where(col <= row, s, -jnp.inf)
        m_prev = m_sc[...]
        m_new = jnp.maximum(m_prev, jnp.max(s, axis=1, keepdims=True))
        alpha = jnp.exp(m_prev - m_new)
        p = jnp.exp(s - m_new)
        l_sc[...] = alpha * l_sc[...] + jnp.sum(p, axis=1, keepdims=True)
        acc_sc[...] = alpha * acc_sc[...] + jnp.dot(
            p.astype(BF16), v_ref[...], preferred_element_type=F32)
        m_sc[...] = m_new

    @pl.when(ki < qi)
    def _():
        step(False)

    @pl.when(ki == qi)
    def _():
        step(True)
        o_ref[...] = (acc_sc[...] / l_sc[...]).astype(o_ref.dtype)


def _forgetting_attention(qkv, f_col, f_row, B, *, blk=1024):
    T = qkv.shape[0]
    W = qkv.shape[1] // 3
    H = W // HEAD_DIM
    S = T // B
    blk = _pick_block(S, blk)
    nq = S // blk
    pairs = [(q, k) for q in range(nq) for k in range(q + 1)]
    qi_tbl = jnp.asarray(np.array([p[0] for p in pairs], np.int32))
    ki_tbl = jnp.asarray(np.array([p[1] for p in pairs], np.int32))
    grid_spec = pltpu.PrefetchScalarGridSpec(
        num_scalar_prefetch=2,
        grid=(B, H, len(pairs)),
        in_specs=[
            pl.BlockSpec((blk, HEAD_DIM), lambda b, h, p, qt, kt: (b * nq + qt[p], h)),
            pl.BlockSpec((blk, HEAD_DIM), lambda b, h, p, qt, kt: (b * nq + kt[p], H + h)),
            pl.BlockSpec((blk, HEAD_DIM), lambda b, h, p, qt, kt: (b * nq + kt[p], 2 * H + h)),
            pl.BlockSpec((blk, LANES), lambda b, h, p, qt, kt: (b * nq + qt[p], 0)),
            pl.BlockSpec((1, 1, 1, blk), lambda b, h, p, qt, kt: (b, h, 0, kt[p])),
        ],
        out_specs=pl.BlockSpec((blk, HEAD_DIM), lambda b, h, p, qt, kt: (b * nq + qt[p], h)),
        scratch_shapes=[
            pltpu.VMEM((blk, 1), F32),
            pltpu.VMEM((blk, 1), F32),
            pltpu.VMEM((blk, HEAD_DIM), F32),
            pltpu.VMEM((blk, 1), F32),
        ],
    )
    return pl.pallas_call(
        functools.partial(_attn_kernel, scale=HEAD_DIM ** -0.5),
        grid_spec=grid_spec,
        out_shape=jax.ShapeDtypeStruct((T, W), BF16),
        compiler_params=_params(("parallel", "parallel", "arbitrary"), 48),
        name="forgetting_attention",
    )(qi_tbl, ki_tbl, qkv, qkv, qkv, f_col, f_row)


def _sgu_kernel(u_ref, v_ref, gv_ref, ws_ref, bs_ref, o_ref, *, groups):
    v = _norm_rows(v_ref[...].astype(F32), gv_ref[...]).astype(BF16)
    bm = v.shape[0]
    t_idx = lax.broadcasted_iota(jnp.int32, (CHUNK, CHUNK), 0)
    s_idx = lax.broadcasted_iota(jnp.int32, (CHUNK, CHUNK), 1)
    causal = s_idx <= t_idx
    for g in range(groups):
        w = jnp.where(causal, ws_ref[g], 0.0).astype(BF16)
        bias = bs_ref[:, g:g + 1]
        cols = slice(g * SGU_DIM, (g + 1) * SGU_DIM)
        for n in range(bm // CHUNK):
            rows = slice(n * CHUNK, (n + 1) * CHUNK)
            mixed = jnp.dot(w, v[rows, cols], preferred_element_type=F32) + bias
            o_ref[rows, cols] = (u_ref[rows, cols].astype(F32) * mixed).astype(o_ref.dtype)


def _chunked_sgu(ug, g_v, w_s, b_s_t, *, bm=512):
    T = ug.shape[0]
    W = ug.shape[1] // 2
    G = W // SGU_DIM
    return pl.pallas_call(
        functools.partial(_sgu_kernel, groups=G),
        grid=(T // bm,),
        in_specs=[
            pl.BlockSpec((bm, W), lambda i: (i, 0)),
            pl.BlockSpec((bm, W), lambda i: (i, 1)),
            pl.BlockSpec((1, W), lambda i: (0, 0)),
            pl.BlockSpec((G, CHUNK, CHUNK), lambda i: (0, 0, 0)),
            pl.BlockSpec((CHUNK, G), lambda i: (0, 0)),
        ],
        out_specs=pl.BlockSpec((bm, W), lambda i: (i, 0)),
        out_shape=jax.ShapeDtypeStruct((T, W), BF16),
        compiler_params=_params(("parallel",), 32),
        name="chunked_sgu",
    )(ug, ug, g_v.reshape(1, W), w_s, b_s_t)


def kernel(x, c, w_mod, b_mod, g_mix, w_in, b_f, g_v, w_s, b_s,
           w_pa, w_pm, w_o, g_ffn, w_up, w_down, g_final):
    B, S, D = x.shape
    L = w_mod.shape[0]
    T = B * S
    W_ATTN = w_pa.shape[1]
    W_SGU = w_pm.shape[1]
    H = W_ATTN // HEAD_DIM
    c_qkv = 3 * W_ATTN
    c_ug = c_qkv + H

    mod = _modulation(c, w_mod, b_mod).reshape(L, B, 6, 1, D)
    xf = x.reshape(T, D)
    for l in range(L):
        sh1, sc1, gt1, sh2, sc2, gt2 = (mod[l, :, i] for i in range(6))
        w_qkv = w_in[l, :, :c_qkv].astype(BF16)
        w_f = jnp.zeros((D, LANES), BF16).at[:, :H].set(w_in[l, :, c_qkv:c_ug].astype(BF16))
        w_ug = w_in[l, :, c_ug:c_ug + 2 * W_SGU].astype(BF16)
        w_gate = w_in[l, :, c_ug + 2 * W_SGU:].astype(BF16)
        bf_pad = jnp.zeros((1, LANES), F32).at[0, :H].set(b_f[l])

        h, f_col = _prenorm_forget(xf, g_mix[l], sc1, sh1, w_f, bf_pad)
        f_row = f_col.reshape(B, S, LANES)[:, :, :H].transpose(0, 2, 1).reshape(B, H, 1, S)
        qkv = _matmul_act(h, w_qkv, "none", name="proj_qkv")
        ug = _matmul_act(h, w_ug, "gelu", name="proj_sgu")
        gates = _matmul_act(h, w_gate, "sigmoid", name="proj_gates")
        a = _forgetting_attention(qkv, f_col, f_row, B)
        m = _chunked_sgu(ug, g_v[l], w_s[l], b_s[l].T)
        y = _branch_merge(a, m, w_pa[l].astype(BF16), w_pm[l].astype(BF16), gates)
        xf = _matmul_residual(y, w_o[l].astype(BF16), xf, gt1, name="proj_out")

        h2 = _prenorm(xf, g_ffn[l], sc2, sh2)
        hid = _matmul_act(h2, w_up[l].astype(BF16), "relu2", name="ffn_up")
        xf = _matmul_residual(hid, w_down[l].astype(BF16), xf, gt2, name="ffn_down")
    return _rmsnorm(xf, g_final).reshape(B, S, D)
```

```python
import functools
import math

import jax
import jax.numpy as jnp
from jax import lax
from jax.experimental import pallas as pl
from jax.experimental.pallas import tpu as pltpu

HEAD_DIM = 128
SGU_DIM = 128
CHUNK = 128
EPS = 1e-6
LANES = 128
SUBLANES = 8

F32 = jnp.float32
BF16 = jnp.bfloat16


def _pick_block(dim, preferred, align=LANES):
    best = None
    for cand in range(align, min(dim, preferred) + 1, align):
        if dim % cand == 0:
            best = cand
    assert best is not None, (dim, preferred, align)
    return best


def _params(dims, vmem_mb):
    return pltpu.CompilerParams(dimension_semantics=dims,
                                vmem_limit_bytes=vmem_mb * 1024 * 1024)


def _mod_kernel(c_ref, w_ref, b_ref, o_ref):
    c = c_ref[...]
    cond = (c * jax.nn.sigmoid(c)).astype(BF16)
    w = w_ref[0].astype(BF16)
    o_ref[0] = jnp.dot(cond, w, preferred_element_type=F32) + b_ref[0]


def _modulation(c, w_mod, b_mod):
    L, D, NM = w_mod.shape
    B = c.shape[0]
    rows = SUBLANES
    c_pad = jnp.zeros((rows, D), F32).at[:B].set(c)
    bn = _pick_block(NM, 512)
    out = pl.pallas_call(
        _mod_kernel,
        grid=(L, NM // bn),
        in_specs=[
            pl.BlockSpec((rows, D), lambda l, j: (0, 0)),
            pl.BlockSpec((1, D, bn), lambda l, j: (l, 0, j)),
            pl.BlockSpec((1, 1, bn), lambda l, j: (l, 0, j)),
        ],
        out_specs=pl.BlockSpec((1, rows, bn), lambda l, j: (l, 0, j)),
        out_shape=jax.ShapeDtypeStruct((L, rows, NM), F32),
        compiler_params=_params(("parallel", "parallel"), 40),
        name="modulation",
    )(c_pad, w_mod, b_mod.reshape(L, 1, NM))
    return out[:, :B]


def _norm_rows(x, g):
    ms = jnp.mean(x * x, axis=-1, keepdims=True)
    return x * lax.rsqrt(ms + EPS) * g


def _prenorm_kernel(x_ref, g_ref, sc_ref, sh_ref, o_ref):
    y = _norm_rows(x_ref[...], g_ref[...])
    o_ref[...] = (y * (1.0 + sc_ref[0]) + sh_ref[0]).astype(o_ref.dtype)


def _rmsnorm_kernel(x_ref, g_ref, o_ref):
    o_ref[...] = _norm_rows(x_ref[...], g_ref[...]).astype(o_ref.dtype)


def _log_sigmoid(x):
    return jnp.minimum(x, 0.0) - jnp.log1p(jnp.exp(-jnp.abs(x)))


def _prenorm_forget_kernel(x_ref, g_ref, sc_ref, sh_ref, wf_ref, bf_ref,
                           h_ref, f_ref, carry_ref, *, blocks_per_seq):
    i = pl.program_id(0)
    y = _norm_rows(x_ref[...], g_ref[...])
    h = (y * (1.0 + sc_ref[0]) + sh_ref[0]).astype(BF16)
    h_ref[...] = h
    zf = jnp.dot(h, wf_ref[...], preferred_element_type=F32) + bf_ref[...]
    run = _log_sigmoid(zf)
    bm = run.shape[0]
    row = lax.broadcasted_iota(jnp.int32, run.shape, 0)
    shift = 1
    while shift < bm:
        run = run + jnp.where(row >= shift, pltpu.roll(run, shift, axis=0), 0.0)
        shift *= 2

    @pl.when(i % blocks_per_seq == 0)
    def _():
        carry_ref[...] = jnp.zeros_like(carry_ref)

    run = run + carry_ref[0:1, :]
    f_ref[...] = run
    carry_ref[...] = jnp.broadcast_to(run[bm - 1:bm, :], carry_ref.shape)


def _prenorm(x, g, sc, sh, *, bm=256):
    T, D = x.shape
    B = sc.shape[0]
    S = T // B
    return pl.pallas_call(
        _prenorm_kernel,
        grid=(T // bm,),
        in_specs=[
            pl.BlockSpec((bm, D), lambda i: (i, 0)),
            pl.BlockSpec((1, D), lambda i: (0, 0)),
            pl.BlockSpec((1, 1, D), lambda i: (i // (S // bm), 0, 0)),
            pl.BlockSpec((1, 1, D), lambda i: (i // (S // bm), 0, 0)),
        ],
        out_specs=pl.BlockSpec((bm, D), lambda i: (i, 0)),
        out_shape=jax.ShapeDtypeStruct((T, D), BF16),
        compiler_params=_params(("parallel",), 32),
        name="prenorm",
    )(x, g.reshape(1, D), sc, sh)


def _prenorm_forget(x, g, sc, sh, w_f, b_f, layer, *, bm=256):
    T, D = x.shape
    B = sc.shape[0]
    S = T // B
    kern = functools.partial(_prenorm_forget_kernel, blocks_per_seq=S // bm)
    return pl.pallas_call(
        kern,
        grid=(T // bm,),
        in_specs=[
            pl.BlockSpec((bm, D), lambda i: (i, 0)),
            pl.BlockSpec((1, D), lambda i: (0, 0)),
            pl.BlockSpec((1, 1, D), lambda i: (i // (S // bm), 0, 0)),
            pl.BlockSpec((1, 1, D), lambda i: (i // (S // bm), 0, 0)),
            pl.BlockSpec((None, D, LANES), lambda i: (layer, 0, 0)),
            pl.BlockSpec((None, 1, LANES), lambda i: (layer, 0, 0)),
        ],
        out_specs=[
            pl.BlockSpec((bm, D), lambda i: (i, 0)),
            pl.BlockSpec((bm, LANES), lambda i: (i, 0)),
        ],
        out_shape=[
            jax.ShapeDtypeStruct((T, D), BF16),
            jax.ShapeDtypeStruct((T, LANES), F32),
        ],
        scratch_shapes=[pltpu.VMEM((SUBLANES, LANES), F32)],
        compiler_params=_params(("arbitrary",), 32),
        name="prenorm_forget",
    )(x, g.reshape(1, D), sc, sh, w_f, b_f)


def _rmsnorm(x, g, *, bm=256):
    T, D = x.shape
    return pl.pallas_call(
        _rmsnorm_kernel,
        grid=(T // bm,),
        in_specs=[
            pl.BlockSpec((bm, D), lambda i: (i, 0)),
            pl.BlockSpec((1, D), lambda i: (0, 0)),
        ],
        out_specs=pl.BlockSpec((bm, D), lambda i: (i, 0)),
        out_shape=jax.ShapeDtypeStruct((T, D), x.dtype),
        compiler_params=_params(("parallel",), 32),
        name="final_rmsnorm",
    )(x, g.reshape(1, D))


def _gelu_tanh(x):
    c = math.sqrt(2.0 / math.pi)
    return 0.5 * x * (1.0 + jnp.tanh(c * (x + 0.044715 * (x * x * x))))


def _apply_act(acc, act):
    if act == "gelu":
        return _gelu_tanh(acc)
    if act == "relu2":
        return jnp.square(jnp.maximum(acc, 0.0))
    if act == "sigmoid":
        return jax.nn.sigmoid(acc)
    return acc


def _mm_act_kernel(x_ref, w_ref, o_ref, *, act):
    acc = jnp.dot(x_ref[...], w_ref[...], preferred_element_type=F32)
    o_ref[...] = _apply_act(acc, act).astype(o_ref.dtype)


def _mm_colscale_kernel(x_ref, w_ref, cs_ref, o_ref):
    acc = jnp.dot(x_ref[...], w_ref[...], preferred_element_type=F32)
    o_ref[...] = (acc * cs_ref[...]).astype(o_ref.dtype)


def _matmul_act(x, w, layer, act, *, n_cols=None, col_off=0, col_scale=None,
                bm=1024, bn=1024, name):
    M, K = x.shape
    N = w.shape[2] if n_cols is None else n_cols
    bm = _pick_block(M, bm)
    bn = _pick_block(N, bn)
    in_specs = [
        pl.BlockSpec((bm, K), lambda i, j: (i, 0)),
        pl.BlockSpec((None, K, bn), lambda i, j: (layer, 0, j + col_off)),
    ]
    args = [x, w]
    if col_scale is None:
        body = functools.partial(_mm_act_kernel, act=act)
    else:
        assert act == "none"
        body = _mm_colscale_kernel
        in_specs.append(pl.BlockSpec((1, bn), lambda i, j: (0, j)))
        args.append(col_scale)
    return pl.pallas_call(
        body,
        grid=(M // bm, N // bn),
        in_specs=in_specs,
        out_specs=pl.BlockSpec((bm, bn), lambda i, j: (i, j)),
        out_shape=jax.ShapeDtypeStruct((M, N), BF16),
        compiler_params=_params(("parallel", "parallel"), 56),
        name=name,
    )(*args)


def _mm_res_kernel(x_ref, w_ref, r_ref, gt_ref, o_ref):
    @pl.when(pl.program_id(2) == 0)
    def _():
        o_ref[...] = r_ref[...]

    o_ref[...] += gt_ref[0] * jnp.dot(x_ref[...], w_ref[...], preferred_element_type=F32)


def _matmul_residual(x, w, layer, res, gate, *, bm=1024, bn=1024, bk=4096, name):
    M, K = x.shape
    N = w.shape[2]
    B = gate.shape[0]
    S = M // B
    bm = _pick_block(S, bm)
    bn = _pick_block(N, bn)
    bk = _pick_block(K, bk)
    return pl.pallas_call(
        _mm_res_kernel,
        grid=(M // bm, N // bn, K // bk),
        in_specs=[
            pl.BlockSpec((bm, bk), lambda i, j, k: (i, k)),
            pl.BlockSpec((None, bk, bn), lambda i, j, k: (layer, k, j)),
            pl.BlockSpec((bm, bn), lambda i, j, k: (i, j)),
            pl.BlockSpec((1, 1, bn), lambda i, j, k: (i // (S // bm), 0, j)),
        ],
        out_specs=pl.BlockSpec((bm, bn), lambda i, j, k: (i, j)),
        out_shape=jax.ShapeDtypeStruct((M, N), F32),
        compiler_params=_params(("parallel", "parallel", "arbitrary"), 56),
        name=name,
    )(x, w, res, gate)


def _merge_kernel(a_ref, m_ref, wa_ref, wm_ref, ga_ref, gm_ref, o_ref):
    pa = jnp.dot(a_ref[...], wa_ref[...], preferred_element_type=F32)
    pm = jnp.dot(m_ref[...], wm_ref[...], preferred_element_type=F32)
    y = ga_ref[...].astype(F32) * pa + gm_ref[...].astype(F32) * pm
    o_ref[...] = y.astype(o_ref.dtype)


def _branch_merge(a, m, w_pa, w_pm, layer, gates, *, bm=1024, bn=512):
    M, Ka = a.shape
    Km = m.shape[1]
    N = w_pa.shape[2]
    bm = _pick_block(M, bm)
    bn = _pick_block(N, bn)
    nb = N // bn
    return pl.pallas_call(
        _merge_kernel,
        grid=(M // bm, nb),
        in_specs=[
            pl.BlockSpec((bm, Ka), lambda i, j: (i, 0)),
            pl.BlockSpec((bm, Km), lambda i, j: (i, 0)),
            pl.BlockSpec((None, Ka, bn), lambda i, j: (layer, 0, j)),
            pl.BlockSpec((None, Km, bn), lambda i, j: (layer, 0, j)),
            pl.BlockSpec((bm, bn), lambda i, j: (i, j)),
            pl.BlockSpec((bm, bn), lambda i, j: (i, j + nb)),
        ],
        out_specs=pl.BlockSpec((bm, bn), lambda i, j: (i, j)),
        out_shape=jax.ShapeDtypeStruct((M, N), BF16),
        compiler_params=_params(("parallel", "parallel"), 56),
        name="branch_merge",
    )(a, m, w_pa, w_pm, gates, gates)


LOG2_E = math.log2(math.e)


def _attn_kernel(q_ref, k_ref, v_ref, fq_ref, fk_ref, o_ref, m_sc, l_sc, acc_sc, *, tk, td):
    h = pl.program_id(1)
    qi = pl.program_id(2)
    tq = q_ref.shape[0]
    per_q = tq // tk

    f_all = fq_ref[...]
    lane = lax.broadcasted_iota(jnp.int32, f_all.shape, 1)
    fq = jnp.broadcast_to(
        jnp.sum(jnp.where(lane == h, f_all, 0.0), axis=1, keepdims=True) * LOG2_E, (tq, LANES))
    m_sc[...] = jnp.full_like(m_sc, -jnp.inf)
    l_sc[...] = jnp.zeros_like(l_sc)
    acc_sc[...] = jnp.zeros_like(acc_sc)

    def tile(r0, kv_chunk, n_chunks, masked):
        rows = slice(r0, tq)
        width = n_chunks * LANES
        kv0 = pl.multiple_of(kv_chunk * LANES, width)
        s = lax.dot_general(q_ref[rows, :], k_ref[pl.ds(kv0, width), :],
                            (((1,), (1,)), ((), ())), preferred_element_type=F32)
        fk = fk_ref[pl.ds(pl.multiple_of(kv_chunk, n_chunks), n_chunks), :] * LOG2_E
        if masked:
            r_idx = lax.broadcasted_iota(jnp.int32, (tq - r0, LANES), 0)
            c_idx = lax.broadcasted_iota(jnp.int32, (tq - r0, LANES), 1)
        chunks = []
        for c in range(n_chunks):
            ch = s[:, c * LANES:(c + 1) * LANES] - fk[c:c + 1, :]
            if masked:
                ch = jnp.where(c_idx + c * LANES <= r_idx, ch, -jnp.inf)
            chunks.append(ch)
        mx = functools.reduce(jnp.maximum, chunks)
        mx = jnp.max(mx, axis=1, keepdims=True)
        fq_r = fq[rows, :]
        m_prev = m_sc[rows, :]
        m_new = jnp.maximum(m_prev, mx + fq_r)
        alpha = jnp.exp2(m_prev - m_new)
        shift = m_new - fq_r
        ps = [jnp.exp2(ch - shift) for ch in chunks]
        l_sc[rows, :] = alpha * l_sc[rows, :] + functools.reduce(lambda a, b: a + b, ps)
        p = jnp.concatenate([x.astype(BF16) for x in ps], axis=1)
        acc_sc[rows, :] = alpha * acc_sc[rows, :] + jnp.dot(
            p, v_ref[pl.ds(kv0, width), :], preferred_element_type=F32)
        m_sc[rows, :] = m_new

    def below_diagonal(it, carry):
        for u in range(per_q):
            tile(0, (it * per_q + u) * (tk // LANES), tk // LANES, False)
        return carry

    lax.fori_loop(0, qi, below_diagonal, 0)

    for d in range(tq // td):
        tile(d * td, qi * (tq // LANES) + d * (td // LANES), td // LANES, True)

    l_tot = jnp.sum(l_sc[...], axis=1, keepdims=True)
    o_ref[...] = (acc_sc[...] / l_tot).astype(o_ref.dtype)


def _forgetting_attention(qkv, f_col, B, *, tq=2048, tk=1024, td=512):
    T = qkv.shape[0]
    W = qkv.shape[1] // 3
    H = W // HEAD_DIM
    S = T // B
    tq = _pick_block(S, tq)
    tk = _pick_block(tq, tk)
    td = _pick_block(tk, td)
    nq = S // tq
    f_row = f_col.reshape(B, S, LANES)[:, :, :H].transpose(0, 2, 1).reshape(B, H, S // LANES, LANES)
    return pl.pallas_call(
        functools.partial(_attn_kernel, tk=tk, td=td),
        grid=(B, H, nq),
        in_specs=[
            pl.BlockSpec((tq, HEAD_DIM), lambda b, h, qi: (b * nq + qi, h)),
            pl.BlockSpec((S, HEAD_DIM), lambda b, h, qi: (b, H + h)),
            pl.BlockSpec((S, HEAD_DIM), lambda b, h, qi: (b, 2 * H + h)),
            pl.BlockSpec((tq, LANES), lambda b, h, qi: (b * nq + qi, 0)),
            pl.BlockSpec((None, None, S // LANES, LANES), lambda b, h, qi: (b, h, 0, 0)),
        ],
        out_specs=pl.BlockSpec((tq, HEAD_DIM), lambda b, h, qi: (b * nq + qi, h)),
        out_shape=jax.ShapeDtypeStruct((T, W), BF16),
        scratch_shapes=[
            pltpu.VMEM((tq, LANES), F32),
            pltpu.VMEM((tq, LANES), F32),
            pltpu.VMEM((tq, HEAD_DIM), F32),
        ],
        compiler_params=_params(("parallel", "parallel", "arbitrary"), 48),
        name="forgetting_attention",
    )(qkv, qkv, qkv, f_col, f_row)


def _sgu_kernel(u_ref, v_ref, gv_ref, ws_ref, bs_ref, o_ref, *, groups):
    v = _norm_rows(v_ref[...].astype(F32), gv_ref[...]).astype(BF16)
    bm = v.shape[0]
    t_idx = lax.broadcasted_iota(jnp.int32, (CHUNK, CHUNK), 0)
    s_idx = lax.broadcasted_iota(jnp.int32, (CHUNK, CHUNK), 1)
    causal = s_idx <= t_idx
    for g in range(groups):
        w = jnp.where(causal, ws_ref[g], 0.0).astype(BF16)
        bias = bs_ref[:, g:g + 1]
        cols = slice(g * SGU_DIM, (g + 1) * SGU_DIM)
        for n in range(bm // CHUNK):
            rows = slice(n * CHUNK, (n + 1) * CHUNK)
            mixed = jnp.dot(w, v[rows, cols], preferred_element_type=F32) + bias
            o_ref[rows, cols] = (u_ref[rows, cols].astype(F32) * mixed).astype(o_ref.dtype)


def _chunked_sgu(ug, g_v, w_s, b_s_t, *, bm=512):
    T = ug.shape[0]
    W = ug.shape[1] // 2
    G = W // SGU_DIM
    return pl.pallas_call(
        functools.partial(_sgu_kernel, groups=G),
        grid=(T // bm,),
        in_specs=[
            pl.BlockSpec((bm, W), lambda i: (i, 0)),
            pl.BlockSpec((bm, W), lambda i: (i, 1)),
            pl.BlockSpec((1, W), lambda i: (0, 0)),
            pl.BlockSpec((G, CHUNK, CHUNK), lambda i: (0, 0, 0)),
            pl.BlockSpec((CHUNK, G), lambda i: (0, 0)),
        ],
        out_specs=pl.BlockSpec((bm, W), lambda i: (i, 0)),
        out_shape=jax.ShapeDtypeStruct((T, W), BF16),
        compiler_params=_params(("parallel",), 32),
        name="chunked_sgu",
    )(ug, ug, g_v.reshape(1, W), w_s, b_s_t)


def kernel(x, c, w_mod, b_mod, g_mix, w_in, b_f, g_v, w_s, b_s,
           w_pa, w_pm, w_o, g_ffn, w_up, w_down, g_final):
    B, S, D = x.shape
    L = w_mod.shape[0]
    T = B * S
    W_ATTN = w_pa.shape[1]
    W_SGU = w_pm.shape[1]
    H = W_ATTN // HEAD_DIM
    c_qkv = 3 * W_ATTN
    c_ug = c_qkv + H

    w_qkv = w_in[:, :, :c_qkv].astype(BF16)
    w_f = jnp.pad(w_in[:, :, c_qkv:c_ug].astype(BF16), ((0, 0), (0, 0), (0, LANES - H)))
    w_rest = w_in[:, :, c_ug:].astype(BF16)
    w_pa16, w_pm16, w_o16 = w_pa.astype(BF16), w_pm.astype(BF16), w_o.astype(BF16)
    w_up16, w_down16 = w_up.astype(BF16), w_down.astype(BF16)
    bf_pad = jnp.pad(b_f, ((0, 0), (0, LANES - H))).reshape(L, 1, LANES)
    q_scale = jnp.concatenate([jnp.full((1, W_ATTN), HEAD_DIM ** -0.5 * LOG2_E, F32),
                               jnp.ones((1, 2 * W_ATTN), F32)], axis=1)

    mod = _modulation(c, w_mod, b_mod).reshape(L, B, 6, 1, D)
    xf = x.reshape(T, D)
    for l in range(L):
        sh1, sc1, gt1, sh2, sc2, gt2 = (mod[l, :, i] for i in range(6))

        h, f_col = _prenorm_forget(xf, g_mix[l], sc1, sh1, w_f, bf_pad, l)
        qkv = _matmul_act(h, w_qkv, l, "none", col_scale=q_scale, name="proj_qkv")
        bn_rest = _pick_block(2 * W_SGU, 1024)
        ug = _matmul_act(h, w_rest, l, "gelu", n_cols=2 * W_SGU, bn=bn_rest, name="proj_sgu")
        gates = _matmul_act(h, w_rest, l, "sigmoid", n_cols=2 * D, bn=bn_rest,
                            col_off=2 * W_SGU // bn_rest, name="proj_gates")
        a = _forgetting_attention(qkv, f_col, B)
        m = _chunked_sgu(ug, g_v[l], w_s[l], b_s[l].T)
        y = _branch_merge(a, m, w_pa16, w_pm16, l, gates)
        xf = _matmul_residual(y, w_o16, l, xf, gt1, name="proj_out")

        h2 = _prenorm(xf, g_ffn[l], sc2, sh2)
        hid = _matmul_act(h2, w_up16, l, "relu2", name="ffn_up")
        xf = _matmul_residual(hid, w_down16, l, xf, gt2, name="ffn_down")
    return _rmsnorm(xf, g_final).reshape(B, S, D)
```

```python
import functools
import math

import jax
import jax.numpy as jnp
from jax import lax
from jax.experimental import pallas as pl
from jax.experimental.pallas import tpu as pltpu

HEAD_DIM = 128
SGU_DIM = 128
CHUNK = 128
EPS = 1e-6
LANES = 128
SUBLANES = 8
VMEM_LIMIT_CAP_BYTES = 60000 * 1024
VMEM_HEADROOM_BYTES = 8 * 1024 * 1024

F32 = jnp.float32
BF16 = jnp.bfloat16


def _pick_block(dim, preferred, align=LANES):
    best = None
    for cand in range(align, min(dim, preferred) + 1, align):
        if dim % cand == 0:
            best = cand
    assert best is not None, (dim, preferred, align)
    return best


def _params(dims, vmem_mb):
    return pltpu.CompilerParams(dimension_semantics=dims,
                                vmem_limit_bytes=vmem_mb * 1024 * 1024)


def _mod_kernel(c_ref, w_ref, b_ref, o_ref):
    c = c_ref[...]
    cond = (c * jax.nn.sigmoid(c)).astype(BF16)
    w = w_ref[0].astype(BF16)
    o_ref[0] = jnp.dot(cond, w, preferred_element_type=F32) + b_ref[0]


def _modulation(c, w_mod, b_mod):
    L, D, NM = w_mod.shape
    B = c.shape[0]
    rows = SUBLANES
    c_pad = jnp.zeros((rows, D), F32).at[:B].set(c)
    bn = _pick_block(NM, 512)
    out = pl.pallas_call(
        _mod_kernel,
        grid=(L, NM // bn),
        in_specs=[
            pl.BlockSpec((rows, D), lambda l, j: (0, 0)),
            pl.BlockSpec((1, D, bn), lambda l, j: (l, 0, j)),
            pl.BlockSpec((1, 1, bn), lambda l, j: (l, 0, j)),
        ],
        out_specs=pl.BlockSpec((1, rows, bn), lambda l, j: (l, 0, j)),
        out_shape=jax.ShapeDtypeStruct((L, rows, NM), F32),
        compiler_params=_params(("parallel", "parallel"), 40),
        name="modulation",
    )(c_pad, w_mod, b_mod.reshape(L, 1, NM))
    return out[:, :B]


def _norm_rows(x, g):
    ms = jnp.mean(x * x, axis=-1, keepdims=True)
    return x * lax.rsqrt(ms + EPS) * g


def _prenorm_kernel(x_ref, g_ref, sc_ref, sh_ref, o_ref):
    y = _norm_rows(x_ref[...], g_ref[...])
    o_ref[...] = (y * (1.0 + sc_ref[0]) + sh_ref[0]).astype(o_ref.dtype)


def _rmsnorm_kernel(x_ref, g_ref, o_ref):
    o_ref[...] = _norm_rows(x_ref[...], g_ref[...]).astype(o_ref.dtype)


def _log_sigmoid(x):
    return jnp.minimum(x, 0.0) - jnp.log1p(jnp.exp(-jnp.abs(x)))


def _prenorm_forget_kernel(x_ref, g_ref, sc_ref, sh_ref, wf_ref, bf_ref,
                           h_ref, f_ref, carry_ref, *, blocks_per_seq):
    i = pl.program_id(0)
    y = _norm_rows(x_ref[...], g_ref[...])
    h = (y * (1.0 + sc_ref[0]) + sh_ref[0]).astype(BF16)
    h_ref[...] = h
    zf = jnp.dot(h, wf_ref[...], preferred_element_type=F32) + bf_ref[...]
    run = _log_sigmoid(zf)
    bm = run.shape[0]
    row = lax.broadcasted_iota(jnp.int32, run.shape, 0)
    shift = 1
    while shift < bm:
        run = run + jnp.where(row >= shift, pltpu.roll(run, shift, axis=0), 0.0)
        shift *= 2

    @pl.when(i % blocks_per_seq == 0)
    def _():
        carry_ref[...] = jnp.zeros_like(carry_ref)

    run = run + carry_ref[0:1, :]
    f_ref[...] = run
    carry_ref[...] = jnp.broadcast_to(run[bm - 1:bm, :], carry_ref.shape)


def _prenorm(x, g, sc, sh, *, bm=256):
    T, D = x.shape
    B = sc.shape[0]
    S = T // B
    return pl.pallas_call(
        _prenorm_kernel,
        grid=(T // bm,),
        in_specs=[
            pl.BlockSpec((bm, D), lambda i: (i, 0)),
            pl.BlockSpec((1, D), lambda i: (0, 0)),
            pl.BlockSpec((1, 1, D), lambda i: (i // (S // bm), 0, 0)),
            pl.BlockSpec((1, 1, D), lambda i: (i // (S // bm), 0, 0)),
        ],
        out_specs=pl.BlockSpec((bm, D), lambda i: (i, 0)),
        out_shape=jax.ShapeDtypeStruct((T, D), BF16),
        compiler_params=_params(("parallel",), 32),
        name="prenorm",
    )(x, g.reshape(1, D), sc, sh)


def _prenorm_forget(x, g, sc, sh, w_f, b_f, layer, *, bm=256):
    T, D = x.shape
    B = sc.shape[0]
    S = T // B
    kern = functools.partial(_prenorm_forget_kernel, blocks_per_seq=S // bm)
    return pl.pallas_call(
        kern,
        grid=(T // bm,),
        in_specs=[
            pl.BlockSpec((bm, D), lambda i: (i, 0)),
            pl.BlockSpec((1, D), lambda i: (0, 0)),
            pl.BlockSpec((1, 1, D), lambda i: (i // (S // bm), 0, 0)),
            pl.BlockSpec((1, 1, D), lambda i: (i // (S // bm), 0, 0)),
            pl.BlockSpec((None, D, LANES), lambda i: (layer, 0, 0)),
            pl.BlockSpec((None, 1, LANES), lambda i: (layer, 0, 0)),
        ],
        out_specs=[
            pl.BlockSpec((bm, D), lambda i: (i, 0)),
            pl.BlockSpec((bm, LANES), lambda i: (i, 0)),
        ],
        out_shape=[
            jax.ShapeDtypeStruct((T, D), BF16),
            jax.ShapeDtypeStruct((T, LANES), F32),
        ],
        scratch_shapes=[pltpu.VMEM((SUBLANES, LANES), F32)],
        compiler_params=_params(("arbitrary",), 32),
        name="prenorm_forget",
    )(x, g.reshape(1, D), sc, sh, w_f, b_f)


def _rmsnorm(x, g, *, bm=256):
    T, D = x.shape
    return pl.pallas_call(
        _rmsnorm_kernel,
        grid=(T // bm,),
        in_specs=[
            pl.BlockSpec((bm, D), lambda i: (i, 0)),
            pl.BlockSpec((1, D), lambda i: (0, 0)),
        ],
        out_specs=pl.BlockSpec((bm, D), lambda i: (i, 0)),
        out_shape=jax.ShapeDtypeStruct((T, D), x.dtype),
        compiler_params=_params(("parallel",), 32),
        name="final_rmsnorm",
    )(x, g.reshape(1, D))


def _gelu_tanh(x):
    c = math.sqrt(2.0 / math.pi)
    return 0.5 * x * (1.0 + jnp.tanh(c * (x + 0.044715 * (x * x * x))))


def _apply_act(acc, act):
    if act == "gelu":
        return _gelu_tanh(acc)
    if act == "relu2":
        return jnp.square(jnp.maximum(acc, 0.0))
    if act == "sigmoid":
        return jax.nn.sigmoid(acc)
    return acc


def _ws_kernel(*refs, n_act, transposed, mode, act):
    acts = refs[:n_act]
    chunk_refs = refs[n_act:2 * n_act]
    n_extra = {"act": 0, "colscale": 1, "residual": 2, "merge": 2}[mode]
    extra = refs[2 * n_act:2 * n_act + n_extra]
    o_ref = refs[2 * n_act + n_extra]
    bufs = refs[2 * n_act + n_extra + 1:]
    j = pl.program_id(0)
    i = pl.program_id(1)

    def stage(parity):
        for t in range(n_act):
            buf = bufs[2 * t + parity]
            if transposed[t]:
                chunk = chunk_refs[t][0].astype(BF16)
                ck = chunk.shape[1]
                buf[:, pl.ds(pl.multiple_of(i * ck, ck), ck)] = chunk
            else:
                chunk = chunk_refs[t][...].astype(BF16)
                ck = chunk.shape[0]
                buf[pl.ds(pl.multiple_of(i * ck, ck), ck), :] = chunk

    def multiply(parity):
        accs = []
        for t in range(n_act):
            w = bufs[2 * t + parity][...]
            dims = (((1,), (1,)), ((), ())) if transposed[t] else (((1,), (0,)), ((), ()))
            accs.append(lax.dot_general(acts[t][...], w, dims, preferred_element_type=F32))
        if mode == "act":
            y = _apply_act(accs[0], act)
        elif mode == "colscale":
            y = accs[0] * extra[0][...]
        elif mode == "residual":
            y = extra[0][...] + extra[1][0] * accs[0]
        else:
            y = extra[0][...].astype(F32) * accs[0] + extra[1][...].astype(F32) * accs[1]
        o_ref[...] = y.astype(o_ref.dtype)

    @pl.when(j == 0)
    def _():
        stage(0)

    for parity in (0, 1):
        @pl.when(jnp.logical_and(j > 0, j % 2 == parity))
        def _():
            stage(parity)
            multiply(1 - parity)


def _ws_matmul(acts, weights, layer, src_col0, n_cols, mode, *, transposed=None, act="none",
               extras=(), rows_per_batch=None, out_dtype=BF16, bm=1024, bn=1024, name):
    n_act = len(acts)
    transposed = tuple(transposed or (False,) * n_act)
    M = acts[0].shape[0]
    bm = _pick_block(M if rows_per_batch is None else rows_per_batch, bm)
    bn = _pick_block(n_cols, bn)
    ni, nj = M // bm, n_cols // bn
    last = nj - 1

    def row_blk(j, i):
        return jnp.where(j == 0, 0, i)

    in_specs, args = [], []
    for a in acts:
        in_specs.append(pl.BlockSpec((bm, a.shape[1]), lambda j, i: (row_blk(j, i), 0)))
        args.append(a)
    scratch = []
    for a, w, tr in zip(acts, weights, transposed):
        K = a.shape[1]
        assert K % ni == 0, (K, ni)
        ck = K // ni
        if tr:
            assert ck % LANES == 0 and src_col0 % SUBLANES == 0, (ck, src_col0)
            in_specs.append(pl.BlockSpec(
                (pl.Element(1), pl.Element(bn), pl.Element(ck)),
                lambda j, i: (layer,
                              pl.multiple_of(src_col0 + jnp.minimum(j, last) * bn, SUBLANES),
                              pl.multiple_of(i * ck, LANES))))
            scratch += [pltpu.VMEM((bn, K), BF16)] * 2
        else:
            assert ck % (2 * SUBLANES) == 0 and src_col0 % bn == 0, (ck, src_col0, bn)
            in_specs.append(pl.BlockSpec(
                (None, ck, bn), lambda j, i: (layer, i, src_col0 // bn + jnp.minimum(j, last))))
            scratch += [pltpu.VMEM((K, bn), BF16)] * 2
        args.append(w)

    def out_blk(j, i):
        return (row_blk(j, i), jnp.maximum(j - 1, 0))

    if mode == "colscale":
        in_specs.append(pl.BlockSpec((1, bn), lambda j, i: (0, jnp.maximum(j - 1, 0))))
    elif mode == "residual":
        in_specs.append(pl.BlockSpec((bm, bn), out_blk))
        per_batch = rows_per_batch // bm
        in_specs.append(pl.BlockSpec(
            (1, 1, bn), lambda j, i: (row_blk(j, i) // per_batch, 0, jnp.maximum(j - 1, 0))))
    elif mode == "merge":
        in_specs.append(pl.BlockSpec((bm, bn), out_blk))
        in_specs.append(pl.BlockSpec(
            (bm, bn), lambda j, i: (row_blk(j, i), jnp.maximum(j - 1, 0) + nj)))
        extras = (extras[0], extras[0])
    args.extend(extras)

    out_bytes = jnp.dtype(out_dtype).itemsize
    vmem = sum(2 * bm * a.shape[1] * 2 + 2 * a.shape[1] * bn * 2 + 2 * (a.shape[1] // ni) * bn * 4
               for a in acts)
    vmem += 2 * bm * bn * out_bytes + n_act * bm * bn * 4
    vmem += {"act": 0, "colscale": 0, "residual": 2 * bm * bn * 4, "merge": 4 * bm * bn * 2}[mode]
    kern = functools.partial(_ws_kernel, n_act=n_act, transposed=transposed, mode=mode, act=act)
    return pl.pallas_call(
        kern,
        grid=(nj + 1, ni),
        in_specs=in_specs,
        out_specs=pl.BlockSpec((bm, bn), out_blk),
        out_shape=jax.ShapeDtypeStruct((M, n_cols), out_dtype),
        scratch_shapes=scratch,
        compiler_params=pltpu.CompilerParams(
            dimension_semantics=("arbitrary", "arbitrary"),
            vmem_limit_bytes=min(vmem + VMEM_HEADROOM_BYTES, VMEM_LIMIT_CAP_BYTES)),
        name=name,
    )(*args)


def _mm_res_kernel(x_ref, w_ref, r_ref, gt_ref, o_ref):
    @pl.when(pl.program_id(2) == 0)
    def _():
        o_ref[...] = r_ref[...]

    o_ref[...] += gt_ref[0] * jnp.dot(x_ref[...], w_ref[...], preferred_element_type=F32)


def _matmul_residual(x, w, layer, res, gate, *, bm=1024, bn=1024, bk=4096, name):
    M, K = x.shape
    N = w.shape[2]
    B = gate.shape[0]
    S = M // B
    bm = _pick_block(S, bm)
    bn = _pick_block(N, bn)
    bk = _pick_block(K, bk)
    return pl.pallas_call(
        _mm_res_kernel,
        grid=(M // bm, N // bn, K // bk),
        in_specs=[
            pl.BlockSpec((bm, bk), lambda i, j, k: (i, k)),
            pl.BlockSpec((None, bk, bn), lambda i, j, k: (layer, k, j)),
            pl.BlockSpec((bm, bn), lambda i, j, k: (i, j)),
            pl.BlockSpec((1, 1, bn), lambda i, j, k: (i // (S // bm), 0, j)),
        ],
        out_specs=pl.BlockSpec((bm, bn), lambda i, j, k: (i, j)),
        out_shape=jax.ShapeDtypeStruct((M, N), F32),
        compiler_params=_params(("parallel", "parallel", "arbitrary"), 56),
        name=name,
    )(x, w, res, gate)


LOG2_E = math.log2(math.e)


def _attn_kernel(q_ref, k_ref, v_ref, fq_ref, fk_ref, o_ref, m_sc, l_sc, acc_sc, *, tk, td):
    h = pl.program_id(1)
    qi = pl.program_id(2)
    tq = q_ref.shape[0]
    per_q = tq // tk

    f_all = fq_ref[...]
    lane = lax.broadcasted_iota(jnp.int32, f_all.shape, 1)
    fq = jnp.broadcast_to(
        jnp.sum(jnp.where(lane == h, f_all, 0.0), axis=1, keepdims=True) * LOG2_E, (tq, LANES))
    m_sc[...] = jnp.full_like(m_sc, -jnp.inf)
    l_sc[...] = jnp.zeros_like(l_sc)
    acc_sc[...] = jnp.zeros_like(acc_sc)

    def tile(r0, kv_chunk, n_chunks, masked):
        rows = slice(r0, tq)
        width = n_chunks * LANES
        kv0 = pl.multiple_of(kv_chunk * LANES, width)
        s = lax.dot_general(q_ref[rows, :], k_ref[pl.ds(kv0, width), :],
                            (((1,), (1,)), ((), ())), preferred_element_type=F32)
        fk = fk_ref[pl.ds(pl.multiple_of(kv_chunk, n_chunks), n_chunks), :] * LOG2_E
        if masked:
            r_idx = lax.broadcasted_iota(jnp.int32, (tq - r0, LANES), 0)
            c_idx = lax.broadcasted_iota(jnp.int32, (tq - r0, LANES), 1)
        chunks = []
        for c in range(n_chunks):
            ch = s[:, c * LANES:(c + 1) * LANES] - fk[c:c + 1, :]
            if masked:
                ch = jnp.where(c_idx + c * LANES <= r_idx, ch, -jnp.inf)
            chunks.append(ch)
        mx = functools.reduce(jnp.maximum, chunks)
        mx = jnp.max(mx, axis=1, keepdims=True)
        fq_r = fq[rows, :]
        m_prev = m_sc[rows, :]
        m_new = jnp.maximum(m_prev, mx + fq_r)
        alpha = jnp.exp2(m_prev - m_new)
        shift = m_new - fq_r
        ps = [jnp.exp2(ch - shift) for ch in chunks]
        l_sc[rows, :] = alpha * l_sc[rows, :] + functools.reduce(lambda a, b: a + b, ps)
        p = jnp.concatenate([x.astype(BF16) for x in ps], axis=1)
        acc_sc[rows, :] = alpha * acc_sc[rows, :] + jnp.dot(
            p, v_ref[pl.ds(kv0, width), :], preferred_element_type=F32)
        m_sc[rows, :] = m_new

    def below_diagonal(it, carry):
        for u in range(per_q):
            tile(0, (it * per_q + u) * (tk // LANES), tk // LANES, False)
        return carry

    lax.fori_loop(0, qi, below_diagonal, 0)

    for d in range(tq // td):
        tile(d * td, qi * (tq // LANES) + d * (td // LANES), td // LANES, True)

    l_tot = jnp.sum(l_sc[...], axis=1, keepdims=True)
    o_ref[...] = (acc_sc[...] / l_tot).astype(o_ref.dtype)


def _forgetting_attention(qkv, f_col, B, *, tq=2048, tk=1024, td=512):
    T = qkv.shape[0]
    W = qkv.shape[1] // 3
    H = W // HEAD_DIM
    S = T // B
    tq = _pick_block(S, tq)
    tk = _pick_block(tq, tk)
    td = _pick_block(tk, td)
    nq = S // tq
    f_row = f_col.reshape(B, S, LANES)[:, :, :H].transpose(0, 2, 1).reshape(B, H, S // LANES, LANES)
    return pl.pallas_call(
        functools.partial(_attn_kernel, tk=tk, td=td),
        grid=(B, H, nq),
        in_specs=[
            pl.BlockSpec((tq, HEAD_DIM), lambda b, h, qi: (b * nq + qi, h)),
            pl.BlockSpec((S, HEAD_DIM), lambda b, h, qi: (b, H + h)),
            pl.BlockSpec((S, HEAD_DIM), lambda b, h, qi: (b, 2 * H + h)),
            pl.BlockSpec((tq, LANES), lambda b, h, qi: (b * nq + qi, 0)),
            pl.BlockSpec((None, None, S // LANES, LANES), lambda b, h, qi: (b, h, 0, 0)),
        ],
        out_specs=pl.BlockSpec((tq, HEAD_DIM), lambda b, h, qi: (b * nq + qi, h)),
        out_shape=jax.ShapeDtypeStruct((T, W), BF16),
        scratch_shapes=[
            pltpu.VMEM((tq, LANES), F32),
            pltpu.VMEM((tq, LANES), F32),
            pltpu.VMEM((tq, HEAD_DIM), F32),
        ],
        compiler_params=_params(("parallel", "parallel", "arbitrary"), 48),
        name="forgetting_attention",
    )(qkv, qkv, qkv, f_col, f_row)


def _sgu_kernel(u_ref, v_ref, gv_ref, ws_ref, bs_ref, o_ref, *, groups):
    v = _norm_rows(v_ref[...].astype(F32), gv_ref[...]).astype(BF16)
    bm = v.shape[0]
    t_idx = lax.broadcasted_iota(jnp.int32, (CHUNK, CHUNK), 0)
    s_idx = lax.broadcasted_iota(jnp.int32, (CHUNK, CHUNK), 1)
    causal = s_idx <= t_idx
    for g in range(groups):
        w = jnp.where(causal, ws_ref[g], 0.0).astype(BF16)
        bias = bs_ref[:, g:g + 1]
        cols = slice(g * SGU_DIM, (g + 1) * SGU_DIM)
        for n in range(bm // CHUNK):
            rows = slice(n * CHUNK, (n + 1) * CHUNK)
            mixed = jnp.dot(w, v[rows, cols], preferred_element_type=F32) + bias
            o_ref[rows, cols] = (u_ref[rows, cols].astype(F32) * mixed).astype(o_ref.dtype)


def _chunked_sgu(ug, g_v, w_s, b_s_t, *, bm=512):
    T = ug.shape[0]
    W = ug.shape[1] // 2
    G = W // SGU_DIM
    return pl.pallas_call(
        functools.partial(_sgu_kernel, groups=G),
        grid=(T // bm,),
        in_specs=[
            pl.BlockSpec((bm, W), lambda i: (i, 0)),
            pl.BlockSpec((bm, W), lambda i: (i, 1)),
            pl.BlockSpec((1, W), lambda i: (0, 0)),
            pl.BlockSpec((G, CHUNK, CHUNK), lambda i: (0, 0, 0)),
            pl.BlockSpec((CHUNK, G), lambda i: (0, 0)),
        ],
        out_specs=pl.BlockSpec((bm, W), lambda i: (i, 0)),
        out_shape=jax.ShapeDtypeStruct((T, W), BF16),
        compiler_params=_params(("parallel",), 32),
        name="chunked_sgu",
    )(ug, ug, g_v.reshape(1, W), w_s, b_s_t)


def kernel(x, c, w_mod, b_mod, g_mix, w_in, b_f, g_v, w_s, b_s,
           w_pa, w_pm, w_o, g_ffn, w_up, w_down, g_final):
    B, S, D = x.shape
    L = w_mod.shape[0]
    T = B * S
    W_ATTN = w_pa.shape[1]
    W_SGU = w_pm.shape[1]
    H = W_ATTN // HEAD_DIM
    c_qkv = 3 * W_ATTN
    c_ug = c_qkv + H

    w_f = jnp.pad(w_in[:, :, c_qkv:c_ug].astype(BF16), ((0, 0), (0, 0), (0, LANES - H)))
    w_down16 = w_down.astype(BF16)
    w_in_t = jnp.swapaxes(w_in, 1, 2)
    bf_pad = jnp.pad(b_f, ((0, 0), (0, LANES - H))).reshape(L, 1, LANES)
    q_scale = jnp.concatenate([jnp.full((1, W_ATTN), HEAD_DIM ** -0.5 * LOG2_E, F32),
                               jnp.ones((1, 2 * W_ATTN), F32)], axis=1)

    mod = _modulation(c, w_mod, b_mod).reshape(L, B, 6, 1, D)
    xf = x.reshape(T, D)
    for l in range(L):
        sh1, sc1, gt1, sh2, sc2, gt2 = (mod[l, :, i] for i in range(6))

        h, f_col = _prenorm_forget(xf, g_mix[l], sc1, sh1, w_f, bf_pad, l)
        qkv = _ws_matmul([h], [w_in_t], l, 0, c_qkv, "colscale", transposed=(True,),
                         extras=(q_scale,), name="proj_qkv")
        ug = _ws_matmul([h], [w_in_t], l, c_ug, 2 * W_SGU, "act", transposed=(True,),
                        act="gelu", name="proj_sgu")
        gates = _ws_matmul([h], [w_in_t], l, c_ug + 2 * W_SGU, 2 * D, "act", transposed=(True,),
                           act="sigmoid", name="proj_gates")
        a = _forgetting_attention(qkv, f_col, B)
        m = _chunked_sgu(ug, g_v[l], w_s[l], b_s[l].T)
        y = _ws_matmul([a, m], [w_pa, w_pm], l, 0, D, "merge", extras=(gates,), name="branch_merge")
        xf = _ws_matmul([y], [w_o], l, 0, D, "residual", extras=(xf, gt1), rows_per_batch=S,
                        out_dtype=F32, name="proj_out")

        h2 = _prenorm(xf, g_ffn[l], sc2, sh2)
        hid = _ws_matmul([h2], [w_up], l, 0, w_up.shape[2], "act", act="relu2", name="ffn_up")
        xf = _matmul_residual(hid, w_down16, l, xf, gt2, name="ffn_down")
    return _rmsnorm(xf, g_final).reshape(B, S, D)
```

```python
import functools
import math

import jax
import jax.numpy as jnp
from jax import lax
from jax.experimental import pallas as pl
from jax.experimental.pallas import tpu as pltpu

HEAD_DIM = 128
SGU_DIM = 128
CHUNK = 128
EPS = 1e-6
LANES = 128
SUBLANES = 8
VMEM_LIMIT_CAP_BYTES = 60000 * 1024
VMEM_HEADROOM_BYTES = 8 * 1024 * 1024

F32 = jnp.float32
BF16 = jnp.bfloat16


def _pick_block(dim, preferred, align=LANES):
    best = None
    for cand in range(align, min(dim, preferred) + 1, align):
        if dim % cand == 0:
            best = cand
    assert best is not None, (dim, preferred, align)
    return best


def _params(dims, vmem_mb):
    return pltpu.CompilerParams(dimension_semantics=dims,
                                vmem_limit_bytes=vmem_mb * 1024 * 1024)


def _mod_kernel(c_ref, w_ref, b_ref, o_ref):
    c = c_ref[...]
    cond = (c * jax.nn.sigmoid(c)).astype(BF16)
    w = w_ref[0].astype(BF16)
    o_ref[0] = jnp.dot(cond, w, preferred_element_type=F32) + b_ref[0]


def _modulation(c, w_mod, b_mod):
    L, D, NM = w_mod.shape
    B = c.shape[0]
    rows = SUBLANES
    c_pad = jnp.zeros((rows, D), F32).at[:B].set(c)
    bn = _pick_block(NM, 512)
    out = pl.pallas_call(
        _mod_kernel,
        grid=(L, NM // bn),
        in_specs=[
            pl.BlockSpec((rows, D), lambda l, j: (0, 0)),
            pl.BlockSpec((1, D, bn), lambda l, j: (l, 0, j)),
            pl.BlockSpec((1, 1, bn), lambda l, j: (l, 0, j)),
        ],
        out_specs=pl.BlockSpec((1, rows, bn), lambda l, j: (l, 0, j)),
        out_shape=jax.ShapeDtypeStruct((L, rows, NM), F32),
        compiler_params=_params(("parallel", "parallel"), 40),
        name="modulation",
    )(c_pad, w_mod, b_mod.reshape(L, 1, NM))
    return out[:, :B]


def _norm_rows(x, g):
    ms = jnp.mean(x * x, axis=-1, keepdims=True)
    return x * lax.rsqrt(ms + EPS) * g


def _prenorm_kernel(x_ref, g_ref, sc_ref, sh_ref, o_ref):
    y = _norm_rows(x_ref[...], g_ref[...])
    o_ref[...] = (y * (1.0 + sc_ref[0]) + sh_ref[0]).astype(o_ref.dtype)


def _rmsnorm_kernel(x_ref, g_ref, o_ref):
    o_ref[...] = _norm_rows(x_ref[...], g_ref[...]).astype(o_ref.dtype)


def _log_sigmoid(x):
    return jnp.minimum(x, 0.0) - jnp.log1p(jnp.exp(-jnp.abs(x)))


def _prenorm_forget_kernel(x_ref, g_ref, sc_ref, sh_ref, wf_ref, bf_ref,
                           h_ref, f_ref, carry_ref, *, blocks_per_seq):
    i = pl.program_id(0)
    y = _norm_rows(x_ref[...], g_ref[...])
    h = (y * (1.0 + sc_ref[0]) + sh_ref[0]).astype(BF16)
    h_ref[...] = h
    zf = jnp.dot(h, wf_ref[...], preferred_element_type=F32) + bf_ref[...]
    run = _log_sigmoid(zf)
    bm = run.shape[0]
    row = lax.broadcasted_iota(jnp.int32, run.shape, 0)
    shift = 1
    while shift < bm:
        run = run + jnp.where(row >= shift, pltpu.roll(run, shift, axis=0), 0.0)
        shift *= 2

    @pl.when(i % blocks_per_seq == 0)
    def _():
        carry_ref[...] = jnp.zeros_like(carry_ref)

    run = run + carry_ref[0:1, :]
    f_ref[...] = run
    carry_ref[...] = jnp.broadcast_to(run[bm - 1:bm, :], carry_ref.shape)


def _prenorm(x, g, sc, sh, *, bm=512):
    T, D = x.shape
    B = sc.shape[0]
    S = T // B
    return pl.pallas_call(
        _prenorm_kernel,
        grid=(T // bm,),
        in_specs=[
            pl.BlockSpec((bm, D), lambda i: (i, 0)),
            pl.BlockSpec((1, D), lambda i: (0, 0)),
            pl.BlockSpec((1, 1, D), lambda i: (i // (S // bm), 0, 0)),
            pl.BlockSpec((1, 1, D), lambda i: (i // (S // bm), 0, 0)),
        ],
        out_specs=pl.BlockSpec((bm, D), lambda i: (i, 0)),
        out_shape=jax.ShapeDtypeStruct((T, D), BF16),
        compiler_params=_params(("parallel",), 48),
        name="prenorm",
    )(x, g.reshape(1, D), sc, sh)


def _prenorm_forget(x, g, sc, sh, w_f, b_f, layer, *, bm=256):
    T, D = x.shape
    B = sc.shape[0]
    S = T // B
    kern = functools.partial(_prenorm_forget_kernel, blocks_per_seq=S // bm)
    return pl.pallas_call(
        kern,
        grid=(T // bm,),
        in_specs=[
            pl.BlockSpec((bm, D), lambda i: (i, 0)),
            pl.BlockSpec((1, D), lambda i: (0, 0)),
            pl.BlockSpec((1, 1, D), lambda i: (i // (S // bm), 0, 0)),
            pl.BlockSpec((1, 1, D), lambda i: (i // (S // bm), 0, 0)),
            pl.BlockSpec((None, D, LANES), lambda i: (layer, 0, 0)),
            pl.BlockSpec((None, 1, LANES), lambda i: (layer, 0, 0)),
        ],
        out_specs=[
            pl.BlockSpec((bm, D), lambda i: (i, 0)),
            pl.BlockSpec((bm, LANES), lambda i: (i, 0)),
        ],
        out_shape=[
            jax.ShapeDtypeStruct((T, D), BF16),
            jax.ShapeDtypeStruct((T, LANES), F32),
        ],
        scratch_shapes=[pltpu.VMEM((SUBLANES, LANES), F32)],
        compiler_params=_params(("arbitrary",), 32),
        name="prenorm_forget",
    )(x, g.reshape(1, D), sc, sh, w_f, b_f)


def _rmsnorm(x, g, *, bm=512):
    T, D = x.shape
    return pl.pallas_call(
        _rmsnorm_kernel,
        grid=(T // bm,),
        in_specs=[
            pl.BlockSpec((bm, D), lambda i: (i, 0)),
            pl.BlockSpec((1, D), lambda i: (0, 0)),
        ],
        out_specs=pl.BlockSpec((bm, D), lambda i: (i, 0)),
        out_shape=jax.ShapeDtypeStruct((T, D), x.dtype),
        compiler_params=_params(("parallel",), 56),
        name="final_rmsnorm",
    )(x, g.reshape(1, D))


def _gelu_tanh(x):
    c = math.sqrt(2.0 / math.pi)
    return 0.5 * x * (1.0 + jnp.tanh(c * (x + 0.044715 * (x * x * x))))


def _apply_act(acc, act):
    if act == "gelu":
        return _gelu_tanh(acc)
    if act == "relu2":
        return jnp.square(jnp.maximum(acc, 0.0))
    if act == "sigmoid":
        return 0.5 * jnp.tanh(0.5 * acc) + 0.5
    return acc


def _ws_kernel(*refs, n_act, transposed, mode, act, side_cast):
    acts = refs[:n_act]
    chunk_refs = refs[n_act:2 * n_act]
    n_extra = {"act": 0, "colscale": 1, "residual": 2, "merge": 2}[mode]
    pos = 2 * n_act + n_extra
    extra = refs[2 * n_act:pos]
    if side_cast:
        side_in, o_ref, side_out = refs[pos:pos + 3]
        bufs = refs[pos + 3:]
    else:
        o_ref = refs[pos]
        bufs = refs[pos + 1:]
    j = pl.program_id(0)
    i = pl.program_id(1)

    def stage(parity):
        if side_cast:
            side_out[...] = side_in[...].astype(BF16)
        for t in range(n_act):
            buf = bufs[2 * t + parity]
            if transposed[t]:
                chunk = chunk_refs[t][0].astype(BF16)
                ck = chunk.shape[1]
                buf[:, pl.ds(pl.multiple_of(i * ck, ck), ck)] = chunk
            else:
                chunk = chunk_refs[t][...].astype(BF16)
                ck = chunk.shape[0]
                buf[pl.ds(pl.multiple_of(i * ck, ck), ck), :] = chunk

    def multiply(parity):
        accs = []
        for t in range(n_act):
            w = bufs[2 * t + parity][...]
            dims = (((1,), (1,)), ((), ())) if transposed[t] else (((1,), (0,)), ((), ()))
            accs.append(lax.dot_general(acts[t][...], w, dims, preferred_element_type=F32))
        if mode == "act":
            y = _apply_act(accs[0], act)
        elif mode == "colscale":
            y = accs[0] * extra[0][...]
        elif mode == "residual":
            y = extra[0][...] + extra[1][0] * accs[0]
        else:
            y = extra[0][...].astype(F32) * accs[0] + extra[1][...].astype(F32) * accs[1]
        o_ref[...] = y.astype(o_ref.dtype)

    @pl.when(j == 0)
    def _():
        stage(0)

    for parity in (0, 1):
        @pl.when(jnp.logical_and(j > 0, j % 2 == parity))
        def _():
            stage(parity)
            multiply(1 - parity)


def _ws_matmul(acts, weights, layer, src_col0, n_cols, mode, *, transposed=None, act="none",
               extras=(), rows_per_batch=None, out_dtype=BF16, side_cast=None, bm=1024, bn=1024,
               name):
    n_act = len(acts)
    transposed = tuple(transposed or (False,) * n_act)
    M = acts[0].shape[0]
    bm = _pick_block(M if rows_per_batch is None else rows_per_batch, bm)
    bn = _pick_block(n_cols, bn)
    ni, nj = M // bm, n_cols // bn
    last = nj - 1

    def row_blk(j, i):
        return jnp.where(j == 0, 0, i)

    in_specs, args = [], []
    for a in acts:
        in_specs.append(pl.BlockSpec((bm, a.shape[1]), lambda j, i: (row_blk(j, i), 0)))
        args.append(a)
    scratch = []
    for a, w, tr in zip(acts, weights, transposed):
        K = a.shape[1]
        assert K % ni == 0, (K, ni)
        ck = K // ni
        if tr:
            assert ck % LANES == 0 and src_col0 % SUBLANES == 0, (ck, src_col0)
            in_specs.append(pl.BlockSpec(
                (pl.Element(1), pl.Element(bn), pl.Element(ck)),
                lambda j, i: (layer,
                              pl.multiple_of(src_col0 + jnp.minimum(j, last) * bn, SUBLANES),
                              pl.multiple_of(i * ck, LANES))))
            scratch += [pltpu.VMEM((bn, K), BF16)] * 2
        else:
            assert ck % (2 * SUBLANES) == 0 and src_col0 % bn == 0, (ck, src_col0, bn)
            in_specs.append(pl.BlockSpec(
                (None, ck, bn), lambda j, i: (layer, i, src_col0 // bn + jnp.minimum(j, last))))
            scratch += [pltpu.VMEM((K, bn), BF16)] * 2
        args.append(w)

    def out_blk(j, i):
        return (row_blk(j, i), jnp.maximum(j - 1, 0))

    if mode == "colscale":
        in_specs.append(pl.BlockSpec((1, bn), lambda j, i: (0, jnp.maximum(j - 1, 0))))
    elif mode == "residual":
        in_specs.append(pl.BlockSpec((bm, bn), out_blk))
        per_batch = rows_per_batch // bm
        in_specs.append(pl.BlockSpec(
            (1, 1, bn), lambda j, i: (row_blk(j, i) // per_batch, 0, jnp.maximum(j - 1, 0))))
    elif mode == "merge":
        in_specs.append(pl.BlockSpec((bm, bn), out_blk))
        in_specs.append(pl.BlockSpec(
            (bm, bn), lambda j, i: (row_blk(j, i), jnp.maximum(j - 1, 0) + nj)))
        extras = (extras[0], extras[0])
    args.extend(extras)

    out_specs = pl.BlockSpec((bm, bn), out_blk)
    out_shape = jax.ShapeDtypeStruct((M, n_cols), out_dtype)
    side_bytes = 0
    if side_cast is not None:
        _, R, C = side_cast.shape
        assert R % (nj * ni) == 0 and (R // (nj * ni)) % (2 * SUBLANES) == 0, (R, nj, ni)
        rs = R // (nj * ni)

        def slab(j, i):
            return jnp.maximum(j - 1, 0) * ni + row_blk(j, i)

        in_specs.append(pl.BlockSpec((None, rs, C), lambda j, i: (layer, slab(j, i), 0)))
        args.append(side_cast)
        out_specs = [out_specs, pl.BlockSpec((rs, C), lambda j, i: (slab(j, i), 0))]
        out_shape = [out_shape, jax.ShapeDtypeStruct((R, C), BF16)]
        side_bytes = 2 * rs * C * (4 + 2)

    out_bytes = jnp.dtype(out_dtype).itemsize
    vmem = sum(2 * bm * a.shape[1] * 2 + 2 * a.shape[1] * bn * 2 + 2 * (a.shape[1] // ni) * bn * 4
               for a in acts)
    vmem += 2 * bm * bn * out_bytes + n_act * bm * bn * 4
    vmem += {"act": 0, "colscale": 0, "residual": 2 * bm * bn * 4, "merge": 4 * bm * bn * 2}[mode]
    vmem += side_bytes
    kern = functools.partial(_ws_kernel, n_act=n_act, transposed=transposed, mode=mode, act=act,
                             side_cast=side_cast is not None)
    return pl.pallas_call(
        kern,
        grid=(nj + 1, ni),
        in_specs=in_specs,
        out_specs=out_specs,
        out_shape=out_shape,
        scratch_shapes=scratch,
        compiler_params=pltpu.CompilerParams(
            dimension_semantics=("arbitrary", "arbitrary"),
            vmem_limit_bytes=min(vmem + VMEM_HEADROOM_BYTES, VMEM_LIMIT_CAP_BYTES)),
        name=name,
    )(*args)


def _mm_res_kernel(x_ref, w_ref, r_ref, gt_ref, o_ref):
    @pl.when(pl.program_id(2) == 0)
    def _():
        o_ref[...] = r_ref[...]

    o_ref[...] += gt_ref[0] * jnp.dot(x_ref[...], w_ref[...], preferred_element_type=F32)


def _matmul_residual(x, w, res, gate, *, bm=1024, bn=1024, bk=4096, name):
    M, K = x.shape
    N = w.shape[1]
    B = gate.shape[0]
    S = M // B
    bm = _pick_block(S, bm)
    bn = _pick_block(N, bn)
    bk = _pick_block(K, bk)
    return pl.pallas_call(
        _mm_res_kernel,
        grid=(M // bm, N // bn, K // bk),
        in_specs=[
            pl.BlockSpec((bm, bk), lambda i, j, k: (i, k)),
            pl.BlockSpec((bk, bn), lambda i, j, k: (k, j)),
            pl.BlockSpec((bm, bn), lambda i, j, k: (i, j)),
            pl.BlockSpec((1, 1, bn), lambda i, j, k: (i // (S // bm), 0, j)),
        ],
        out_specs=pl.BlockSpec((bm, bn), lambda i, j, k: (i, j)),
        out_shape=jax.ShapeDtypeStruct((M, N), F32),
        compiler_params=_params(("parallel", "parallel", "arbitrary"), 56),
        name=name,
    )(x, w, res, gate)


LOG2_E = math.log2(math.e)


def _attn_kernel(q_ref, k_ref, v_ref, fq_ref, fk_ref, o_ref, m_sc, l_sc, acc_sc, *, tk, td):
    hg = pl.program_id(1)
    qi = pl.program_id(2)
    tq = q_ref.shape[0]
    heads = q_ref.shape[1] // HEAD_DIM
    per_q = tq // tk

    f_all = fq_ref[...]
    lane = lax.broadcasted_iota(jnp.int32, f_all.shape, 1)
    fqs = [jnp.broadcast_to(
        jnp.sum(jnp.where(lane == hg * heads + hh, f_all, 0.0), axis=1, keepdims=True) * LOG2_E,
        (tq, LANES)) for hh in range(heads)]
    m_sc[...] = jnp.full_like(m_sc, -jnp.inf)
    l_sc[...] = jnp.zeros_like(l_sc)
    acc_sc[...] = jnp.zeros_like(acc_sc)

    def tile(hh, r0, kv_chunk, n_chunks, masked):
        rows = slice(r0, tq)
        cols = slice(hh * HEAD_DIM, (hh + 1) * HEAD_DIM)
        fq = fqs[hh]
        width = n_chunks * LANES
        kv0 = pl.multiple_of(kv_chunk * LANES, width)
        s = lax.dot_general(q_ref[rows, cols], k_ref[pl.ds(kv0, width), cols],
                            (((1,), (1,)), ((), ())), preferred_element_type=F32)
        fk = fk_ref[hh, pl.ds(pl.multiple_of(kv_chunk, n_chunks), n_chunks), :] * LOG2_E
        if masked:
            r_idx = lax.broadcasted_iota(jnp.int32, (LANES, LANES), 0)
            c_idx = lax.broadcasted_iota(jnp.int32, (LANES, LANES), 1)
        chunks = []
        for c in range(n_chunks):
            ch = s[:, c * LANES:(c + 1) * LANES] - fk[c:c + 1, :]
            if masked:
                top, bot = c * LANES, (c + 1) * LANES
                parts = [jnp.where(c_idx <= r_idx, ch[top:bot], -jnp.inf)]
                if top:
                    parts.insert(0, jnp.full((top, LANES), -jnp.inf, F32))
                if bot < ch.shape[0]:
                    parts.append(ch[bot:])
                ch = jnp.concatenate(parts, axis=0)
            chunks.append(ch)
        mx = functools.reduce(jnp.maximum, chunks)
        mx = jnp.max(mx, axis=1, keepdims=True)
        fq_r = fq[rows, :]
        m_prev = m_sc[hh, rows, :]
        m_new = jnp.maximum(m_prev, mx + fq_r)
        alpha = jnp.exp2(m_prev - m_new)
        shift = m_new - fq_r
        ps = [jnp.exp2(ch - shift) for ch in chunks]
        l_sc[hh, rows, :] = alpha * l_sc[hh, rows, :] + functools.reduce(lambda a, b: a + b, ps)
        p = jnp.concatenate([x.astype(BF16) for x in ps], axis=1)
        acc_sc[hh, rows, :] = alpha * acc_sc[hh, rows, :] + jnp.dot(
            p, v_ref[pl.ds(kv0, width), cols], preferred_element_type=F32)
        m_sc[hh, rows, :] = m_new

    def below_diagonal(it, carry):
        for u in range(per_q):
            for hh in range(heads):
                tile(hh, 0, (it * per_q + u) * (tk // LANES), tk // LANES, False)
        return carry

    lax.fori_loop(0, qi, below_diagonal, 0)

    for d in range(tq // td):
        for hh in range(heads):
            tile(hh, d * td, qi * (tq // LANES) + d * (td // LANES), td // LANES, True)

    for hh in range(heads):
        l_tot = jnp.sum(l_sc[hh], axis=1, keepdims=True)
        o_ref[:, hh * HEAD_DIM:(hh + 1) * HEAD_DIM] = (acc_sc[hh] / l_tot).astype(o_ref.dtype)


def _forgetting_attention(qkv, f_col, B, *, tq=2048, tk=1024, td=512, heads_per_step=2):
    T = qkv.shape[0]
    W = qkv.shape[1] // 3
    H = W // HEAD_DIM
    S = T // B
    tq = _pick_block(S, tq)
    tk = _pick_block(tq, tk)
    td = _pick_block(tk, td)
    nq = S // tq
    hp = heads_per_step if H % heads_per_step == 0 else 1
    G = H // hp
    wide = hp * HEAD_DIM
    f_row = f_col.reshape(B, S, LANES)[:, :, :H].transpose(0, 2, 1).reshape(B, G, hp, S // LANES, LANES)
    return pl.pallas_call(
        functools.partial(_attn_kernel, tk=tk, td=td),
        grid=(B, G, nq),
        in_specs=[
            pl.BlockSpec((tq, wide), lambda b, g, qi: (b * nq + qi, g)),
            pl.BlockSpec((S, wide), lambda b, g, qi: (b, G + g)),
            pl.BlockSpec((S, wide), lambda b, g, qi: (b, 2 * G + g)),
            pl.BlockSpec((tq, LANES), lambda b, g, qi: (b * nq + qi, 0)),
            pl.BlockSpec((None, None, hp, S // LANES, LANES), lambda b, g, qi: (b, g, 0, 0, 0)),
        ],
        out_specs=pl.BlockSpec((tq, wide), lambda b, g, qi: (b * nq + qi, g)),
        out_shape=jax.ShapeDtypeStruct((T, W), BF16),
        scratch_shapes=[
            pltpu.VMEM((hp, tq, LANES), F32),
            pltpu.VMEM((hp, tq, LANES), F32),
            pltpu.VMEM((hp, tq, HEAD_DIM), F32),
        ],
        compiler_params=_params(("parallel", "parallel", "arbitrary"), 56),
        name="forgetting_attention",
    )(qkv, qkv, qkv, f_col, f_row)


def _sgu_kernel(u_ref, v_ref, gv_ref, ws_ref, bs_ref, o_ref, *, groups):
    v = _norm_rows(v_ref[...].astype(F32), gv_ref[...]).astype(BF16)
    bm = v.shape[0]
    t_idx = lax.broadcasted_iota(jnp.int32, (CHUNK, CHUNK), 0)
    s_idx = lax.broadcasted_iota(jnp.int32, (CHUNK, CHUNK), 1)
    causal = s_idx <= t_idx
    for g in range(groups):
        w = jnp.where(causal, ws_ref[g], 0.0).astype(BF16)
        bias = bs_ref[:, g:g + 1]
        cols = slice(g * SGU_DIM, (g + 1) * SGU_DIM)
        for n in range(bm // CHUNK):
            rows = slice(n * CHUNK, (n + 1) * CHUNK)
            mixed = jnp.dot(w, v[rows, cols], preferred_element_type=F32) + bias
            o_ref[rows, cols] = (u_ref[rows, cols].astype(F32) * mixed).astype(o_ref.dtype)


def _chunked_sgu(ug, g_v, w_s, b_s_t, *, bm=512):
    T = ug.shape[0]
    W = ug.shape[1] // 2
    G = W // SGU_DIM
    return pl.pallas_call(
        functools.partial(_sgu_kernel, groups=G),
        grid=(T // bm,),
        in_specs=[
            pl.BlockSpec((bm, W), lambda i: (i, 0)),
            pl.BlockSpec((bm, W), lambda i: (i, 1)),
            pl.BlockSpec((1, W), lambda i: (0, 0)),
            pl.BlockSpec((G, CHUNK, CHUNK), lambda i: (0, 0, 0)),
            pl.BlockSpec((CHUNK, G), lambda i: (0, 0)),
        ],
        out_specs=pl.BlockSpec((bm, W), lambda i: (i, 0)),
        out_shape=jax.ShapeDtypeStruct((T, W), BF16),
        compiler_params=_params(("parallel",), 32),
        name="chunked_sgu",
    )(ug, ug, g_v.reshape(1, W), w_s, b_s_t)


def kernel(x, c, w_mod, b_mod, g_mix, w_in, b_f, g_v, w_s, b_s,
           w_pa, w_pm, w_o, g_ffn, w_up, w_down, g_final):
    B, S, D = x.shape
    L = w_mod.shape[0]
    T = B * S
    W_ATTN = w_pa.shape[1]
    W_SGU = w_pm.shape[1]
    H = W_ATTN // HEAD_DIM
    c_qkv = 3 * W_ATTN
    c_ug = c_qkv + H

    w_f = jnp.pad(w_in[:, :, c_qkv:c_ug].astype(BF16), ((0, 0), (0, 0), (0, LANES - H)))
    w_in_t = jnp.swapaxes(w_in, 1, 2)
    bf_pad = jnp.pad(b_f, ((0, 0), (0, LANES - H))).reshape(L, 1, LANES)
    q_scale = jnp.concatenate([jnp.full((1, W_ATTN), HEAD_DIM ** -0.5 * LOG2_E, F32),
                               jnp.ones((1, 2 * W_ATTN), F32)], axis=1)

    mod = _modulation(c, w_mod, b_mod).reshape(L, B, 6, 1, D)
    xf = x.reshape(T, D)
    for l in range(L):
        sh1, sc1, gt1, sh2, sc2, gt2 = (mod[l, :, i] for i in range(6))

        h, f_col = _prenorm_forget(xf, g_mix[l], sc1, sh1, w_f, bf_pad, l)
        qkv = _ws_matmul([h], [w_in_t], l, 0, c_qkv, "colscale", transposed=(True,),
                         extras=(q_scale,), name="proj_qkv")
        ug = _ws_matmul([h], [w_in_t], l, c_ug, 2 * W_SGU, "act", transposed=(True,),
                        act="gelu", name="proj_sgu")
        gates = _ws_matmul([h], [w_in_t], l, c_ug + 2 * W_SGU, 2 * D, "act", transposed=(True,),
                           act="sigmoid", name="proj_gates")
        a = _forgetting_attention(qkv, f_col, B)
        m = _chunked_sgu(ug, g_v[l], w_s[l], b_s[l].T)
        y = _ws_matmul([a, m], [w_pa, w_pm], l, 0, D, "merge", extras=(gates,), name="branch_merge")
        xf = _ws_matmul([y], [w_o], l, 0, D, "residual", extras=(xf, gt1), rows_per_batch=S,
                        out_dtype=F32, name="proj_out")

        h2 = _prenorm(xf, g_ffn[l], sc2, sh2)
        hid, w_down16 = _ws_matmul([h2], [w_up], l, 0, w_up.shape[2], "act", act="relu2",
                                   side_cast=w_down, name="ffn_up")
        xf = _matmul_residual(hid, w_down16, xf, gt2, name="ffn_down")
    return _rmsnorm(xf, g_final).reshape(B, S, D)
```

```python
import functools
import math

import jax
import jax.numpy as jnp
from jax import lax
from jax.experimental import pallas as pl
from jax.experimental.pallas import tpu as pltpu

HEAD_DIM = 128
SGU_DIM = 128
CHUNK = 128
EPS = 1e-6
LANES = 128
SUBLANES = 8
VMEM_LIMIT_CAP_BYTES = 60000 * 1024
VMEM_HEADROOM_BYTES = 8 * 1024 * 1024

F32 = jnp.float32
BF16 = jnp.bfloat16


def _pick_block(dim, preferred, align=LANES):
    best = None
    for cand in range(align, min(dim, preferred) + 1, align):
        if dim % cand == 0:
            best = cand
    assert best is not None, (dim, preferred, align)
    return best


def _params(dims, vmem_mb):
    return pltpu.CompilerParams(dimension_semantics=dims,
                                vmem_limit_bytes=vmem_mb * 1024 * 1024)


def _mod_kernel(c_ref, w_ref, b_ref, o_ref):
    c = c_ref[...]
    cond = (c * jax.nn.sigmoid(c)).astype(BF16)
    w = w_ref[0].astype(BF16)
    o_ref[0] = jnp.dot(cond, w, preferred_element_type=F32) + b_ref[0]


def _modulation(c, w_mod, b_mod):
    L, D, NM = w_mod.shape
    B = c.shape[0]
    rows = SUBLANES
    c_pad = jnp.zeros((rows, D), F32).at[:B].set(c)
    bn = _pick_block(NM, 512)
    out = pl.pallas_call(
        _mod_kernel,
        grid=(L, NM // bn),
        in_specs=[
            pl.BlockSpec((rows, D), lambda l, j: (0, 0)),
            pl.BlockSpec((1, D, bn), lambda l, j: (l, 0, j)),
            pl.BlockSpec((1, 1, bn), lambda l, j: (l, 0, j)),
        ],
        out_specs=pl.BlockSpec((1, rows, bn), lambda l, j: (l, 0, j)),
        out_shape=jax.ShapeDtypeStruct((L, rows, NM), F32),
        compiler_params=_params(("parallel", "parallel"), 40),
        name="modulation",
    )(c_pad, w_mod, b_mod.reshape(L, 1, NM))
    return out[:, :B]


def _norm_rows(x, g):
    ms = jnp.mean(x * x, axis=-1, keepdims=True)
    return x * lax.rsqrt(ms + EPS) * g


NORM_ROW_GROUP = 2 * SUBLANES


def _norm_by_row_groups(x_ref, g_ref, o_ref, sc_ref=None, sh_ref=None):
    D = x_ref.shape[1]
    piece = _pick_block(D, 4 * LANES)
    for r in range(0, x_ref.shape[0], NORM_ROW_GROUP):
        rows = slice(r, r + NORM_ROW_GROUP)
        x = x_ref[rows, :]
        inv = lax.rsqrt(jnp.mean(x * x, axis=-1, keepdims=True) + EPS)
        for c in range(0, D, piece):
            cols = slice(c, c + piece)
            y = x_ref[rows, cols] * inv * g_ref[:, cols]
            if sc_ref is not None:
                y = y * (1.0 + sc_ref[0, :, cols]) + sh_ref[0, :, cols]
            o_ref[rows, cols] = y.astype(o_ref.dtype)


def _prenorm_kernel(x_ref, g_ref, sc_ref, sh_ref, o_ref):
    _norm_by_row_groups(x_ref, g_ref, o_ref, sc_ref, sh_ref)


def _rmsnorm_kernel(x_ref, g_ref, o_ref):
    _norm_by_row_groups(x_ref, g_ref, o_ref)


def _log_sigmoid(x):
    return jnp.minimum(x, 0.0) - jnp.log1p(jnp.exp(-jnp.abs(x)))


def _prenorm_forget_kernel(x_ref, g_ref, sc_ref, sh_ref, wf_ref, bf_ref,
                           h_ref, f_ref, carry_ref, *, blocks_per_seq):
    i = pl.program_id(0)
    y = _norm_rows(x_ref[...], g_ref[...])
    h = (y * (1.0 + sc_ref[0]) + sh_ref[0]).astype(BF16)
    h_ref[...] = h
    zf = jnp.dot(h, wf_ref[...], preferred_element_type=F32) + bf_ref[...]
    run = _log_sigmoid(zf)
    bm = run.shape[0]
    row = lax.broadcasted_iota(jnp.int32, run.shape, 0)
    shift = 1
    while shift < bm:
        run = run + jnp.where(row >= shift, pltpu.roll(run, shift, axis=0), 0.0)
        shift *= 2

    @pl.when(i % blocks_per_seq == 0)
    def _():
        carry_ref[...] = jnp.zeros_like(carry_ref)

    run = run + carry_ref[0:1, :]
    f_ref[...] = run
    carry_ref[...] = jnp.broadcast_to(run[bm - 1:bm, :], carry_ref.shape)


def _prenorm(x, g, sc, sh, *, bm=512):
    T, D = x.shape
    B = sc.shape[0]
    S = T // B
    return pl.pallas_call(
        _prenorm_kernel,
        grid=(T // bm,),
        in_specs=[
            pl.BlockSpec((bm, D), lambda i: (i, 0)),
            pl.BlockSpec((1, D), lambda i: (0, 0)),
            pl.BlockSpec((1, 1, D), lambda i: (i // (S // bm), 0, 0)),
            pl.BlockSpec((1, 1, D), lambda i: (i // (S // bm), 0, 0)),
        ],
        out_specs=pl.BlockSpec((bm, D), lambda i: (i, 0)),
        out_shape=jax.ShapeDtypeStruct((T, D), BF16),
        compiler_params=_params(("parallel",), 48),
        name="prenorm",
    )(x, g.reshape(1, D), sc, sh)


def _prenorm_forget(x, g, sc, sh, w_f, b_f, layer, *, bm=256):
    T, D = x.shape
    B = sc.shape[0]
    S = T // B
    kern = functools.partial(_prenorm_forget_kernel, blocks_per_seq=S // bm)
    return pl.pallas_call(
        kern,
        grid=(T // bm,),
        in_specs=[
            pl.BlockSpec((bm, D), lambda i: (i, 0)),
            pl.BlockSpec((1, D), lambda i: (0, 0)),
            pl.BlockSpec((1, 1, D), lambda i: (i // (S // bm), 0, 0)),
            pl.BlockSpec((1, 1, D), lambda i: (i // (S // bm), 0, 0)),
            pl.BlockSpec((None, D, LANES), lambda i: (layer, 0, 0)),
            pl.BlockSpec((None, 1, LANES), lambda i: (layer, 0, 0)),
        ],
        out_specs=[
            pl.BlockSpec((bm, D), lambda i: (i, 0)),
            pl.BlockSpec((bm, LANES), lambda i: (i, 0)),
        ],
        out_shape=[
            jax.ShapeDtypeStruct((T, D), BF16),
            jax.ShapeDtypeStruct((T, LANES), F32),
        ],
        scratch_shapes=[pltpu.VMEM((SUBLANES, LANES), F32)],
        compiler_params=_params(("arbitrary",), 32),
        name="prenorm_forget",
    )(x, g.reshape(1, D), sc, sh, w_f, b_f)


def _rmsnorm(x, g, *, bm=512):
    T, D = x.shape
    return pl.pallas_call(
        _rmsnorm_kernel,
        grid=(T // bm,),
        in_specs=[
            pl.BlockSpec((bm, D), lambda i: (i, 0)),
            pl.BlockSpec((1, D), lambda i: (0, 0)),
        ],
        out_specs=pl.BlockSpec((bm, D), lambda i: (i, 0)),
        out_shape=jax.ShapeDtypeStruct((T, D), x.dtype),
        compiler_params=_params(("parallel",), 56),
        name="final_rmsnorm",
    )(x, g.reshape(1, D))


def _gelu_tanh(x):
    c = math.sqrt(2.0 / math.pi)
    return 0.5 * x * (1.0 + jnp.tanh(c * (x + 0.044715 * (x * x * x))))


def _apply_act(acc, act):
    if act == "gelu":
        return _gelu_tanh(acc)
    if act == "relu2":
        return jnp.square(jnp.maximum(acc, 0.0))
    if act == "sigmoid":
        return 0.5 * jnp.tanh(0.5 * acc) + 0.5
    return acc


def _ws_kernel(*refs, n_act, transposed, mode, act, side_cast):
    acts = refs[:n_act]
    chunk_refs = refs[n_act:2 * n_act]
    n_extra = {"act": 0, "colscale": 1, "residual": 2, "merge": 2}[mode]
    pos = 2 * n_act + n_extra
    extra = refs[2 * n_act:pos]
    if side_cast:
        side_in, o_ref, side_out = refs[pos:pos + 3]
        bufs = refs[pos + 3:]
    else:
        o_ref = refs[pos]
        bufs = refs[pos + 1:]
    j = pl.program_id(0)
    i = pl.program_id(1)

    def stage(parity):
        if side_cast:
            side_out[...] = side_in[...].astype(BF16)
        for t in range(n_act):
            buf = bufs[2 * t + parity]
            if transposed[t]:
                chunk = chunk_refs[t][0].astype(BF16)
                ck = chunk.shape[1]
                buf[:, pl.ds(pl.multiple_of(i * ck, ck), ck)] = chunk
            else:
                chunk = chunk_refs[t][...].astype(BF16)
                ck = chunk.shape[0]
                buf[pl.ds(pl.multiple_of(i * ck, ck), ck), :] = chunk

    def multiply(parity):
        accs = []
        for t in range(n_act):
            w = bufs[2 * t + parity][...]
            dims = (((1,), (1,)), ((), ())) if transposed[t] else (((1,), (0,)), ((), ()))
            accs.append(lax.dot_general(acts[t][...], w, dims, preferred_element_type=F32))
        if mode == "act":
            y = _apply_act(accs[0], act)
        elif mode == "colscale":
            y = accs[0] * extra[0][...]
        elif mode == "residual":
            y = extra[0][...] + extra[1][0] * accs[0]
        else:
            y = extra[0][...].astype(F32) * accs[0] + extra[1][...].astype(F32) * accs[1]
        o_ref[...] = y.astype(o_ref.dtype)

    @pl.when(j == 0)
    def _():
        stage(0)

    for parity in (0, 1):
        @pl.when(jnp.logical_and(j > 0, j % 2 == parity))
        def _():
            stage(parity)
            multiply(1 - parity)


def _ws_matmul(acts, weights, layer, src_col0, n_cols, mode, *, transposed=None, act="none",
               extras=(), rows_per_batch=None, out_dtype=BF16, side_cast=None, bm=1024, bn=1024,
               name):
    n_act = len(acts)
    transposed = tuple(transposed or (False,) * n_act)
    M = acts[0].shape[0]
    bm = _pick_block(M if rows_per_batch is None else rows_per_batch, bm)
    bn = _pick_block(n_cols, bn)
    ni, nj = M // bm, n_cols // bn
    last = nj - 1

    def row_blk(j, i):
        return jnp.where(j == 0, 0, i)

    in_specs, args = [], []
    for a in acts:
        in_specs.append(pl.BlockSpec((bm, a.shape[1]), lambda j, i: (row_blk(j, i), 0)))
        args.append(a)
    scratch = []
    for a, w, tr in zip(acts, weights, transposed):
        K = a.shape[1]
        assert K % ni == 0, (K, ni)
        ck = K // ni
        if tr:
            assert ck % LANES == 0 and src_col0 % SUBLANES == 0, (ck, src_col0)
            in_specs.append(pl.BlockSpec(
                (pl.Element(1), pl.Element(bn), pl.Element(ck)),
                lambda j, i: (layer,
                              pl.multiple_of(src_col0 + jnp.minimum(j, last) * bn, SUBLANES),
                              pl.multiple_of(i * ck, LANES))))
            scratch += [pltpu.VMEM((bn, K), BF16)] * 2
        else:
            assert ck % (2 * SUBLANES) == 0 and src_col0 % bn == 0, (ck, src_col0, bn)
            in_specs.append(pl.BlockSpec(
                (None, ck, bn), lambda j, i: (layer, i, src_col0 // bn + jnp.minimum(j, last))))
            scratch += [pltpu.VMEM((K, bn), BF16)] * 2
        args.append(w)

    def out_blk(j, i):
        return (row_blk(j, i), jnp.maximum(j - 1, 0))

    if mode == "colscale":
        in_specs.append(pl.BlockSpec((1, bn), lambda j, i: (0, jnp.maximum(j - 1, 0))))
    elif mode == "residual":
        in_specs.append(pl.BlockSpec((bm, bn), out_blk))
        per_batch = rows_per_batch // bm
        in_specs.append(pl.BlockSpec(
            (1, 1, bn), lambda j, i: (row_blk(j, i) // per_batch, 0, jnp.maximum(j - 1, 0))))
    elif mode == "merge":
        in_specs.append(pl.BlockSpec((bm, bn), out_blk))
        in_specs.append(pl.BlockSpec(
            (bm, bn), lambda j, i: (row_blk(j, i), jnp.maximum(j - 1, 0) + nj)))
        extras = (extras[0], extras[0])
    args.extend(extras)

    out_specs = pl.BlockSpec((bm, bn), out_blk)
    out_shape = jax.ShapeDtypeStruct((M, n_cols), out_dtype)
    side_bytes = 0
    if side_cast is not None:
        _, R, C = side_cast.shape
        assert R % (nj * ni) == 0 and (R // (nj * ni)) % (2 * SUBLANES) == 0, (R, nj, ni)
        rs = R // (nj * ni)

        def slab(j, i):
            return jnp.maximum(j - 1, 0) * ni + row_blk(j, i)

        in_specs.append(pl.BlockSpec((None, rs, C), lambda j, i: (layer, slab(j, i), 0)))
        args.append(side_cast)
        out_specs = [out_specs, pl.BlockSpec((rs, C), lambda j, i: (slab(j, i), 0))]
        out_shape = [out_shape, jax.ShapeDtypeStruct((R, C), BF16)]
        side_bytes = 2 * rs * C * (4 + 2)

    out_bytes = jnp.dtype(out_dtype).itemsize
    vmem = sum(2 * bm * a.shape[1] * 2 + 2 * a.shape[1] * bn * 2 + 2 * (a.shape[1] // ni) * bn * 4
               for a in acts)
    vmem += 2 * bm * bn * out_bytes + n_act * bm * bn * 4
    vmem += {"act": 0, "colscale": 0, "residual": 2 * bm * bn * 4, "merge": 4 * bm * bn * 2}[mode]
    vmem += side_bytes
    kern = functools.partial(_ws_kernel, n_act=n_act, transposed=transposed, mode=mode, act=act,
                             side_cast=side_cast is not None)
    return pl.pallas_call(
        kern,
        grid=(nj + 1, ni),
        in_specs=in_specs,
        out_specs=out_specs,
        out_shape=out_shape,
        scratch_shapes=scratch,
        compiler_params=pltpu.CompilerParams(
            dimension_semantics=("arbitrary", "arbitrary"),
            vmem_limit_bytes=min(vmem + VMEM_HEADROOM_BYTES, VMEM_LIMIT_CAP_BYTES)),
        name=name,
    )(*args)


def _mm_res_kernel(x_ref, w_ref, r_ref, gt_ref, o_ref):
    def gated_product():
        return gt_ref[0] * jnp.dot(x_ref[...], w_ref[...], preferred_element_type=F32)

    @pl.when(pl.program_id(2) == 0)
    def _():
        o_ref[...] = r_ref[...] + gated_product()

    @pl.when(pl.program_id(2) > 0)
    def _():
        o_ref[...] += gated_product()


def _matmul_residual(x, w, res, gate, *, bm=1024, bn=1024, bk=4096, name):
    M, K = x.shape
    N = w.shape[1]
    B = gate.shape[0]
    S = M // B
    bm = _pick_block(S, bm)
    bn = _pick_block(N, bn)
    bk = _pick_block(K, bk)
    return pl.pallas_call(
        _mm_res_kernel,
        grid=(M // bm, N // bn, K // bk),
        in_specs=[
            pl.BlockSpec((bm, bk), lambda i, j, k: (i, k)),
            pl.BlockSpec((bk, bn), lambda i, j, k: (k, j)),
            pl.BlockSpec((bm, bn), lambda i, j, k: (i, j)),
            pl.BlockSpec((1, 1, bn), lambda i, j, k: (i // (S // bm), 0, j)),
        ],
        out_specs=pl.BlockSpec((bm, bn), lambda i, j, k: (i, j)),
        out_shape=jax.ShapeDtypeStruct((M, N), F32),
        compiler_params=_params(("parallel", "parallel", "arbitrary"), 56),
        name=name,
    )(x, w, res, gate)


LOG2_E = math.log2(math.e)


def _attn_kernel(q_ref, k_ref, v_ref, fq_ref, fk_ref, o_ref, m_sc, l_sc, acc_sc, *, tk, td):
    hg = pl.program_id(1)
    qi = pl.program_id(2)
    tq = q_ref.shape[0]
    heads = q_ref.shape[1] // HEAD_DIM
    per_q = tq // tk

    f_all = fq_ref[...]
    lane = lax.broadcasted_iota(jnp.int32, f_all.shape, 1)
    fqs = [jnp.broadcast_to(
        jnp.sum(jnp.where(lane == hg * heads + hh, f_all, 0.0), axis=1, keepdims=True) * LOG2_E,
        (tq, LANES)) for hh in range(heads)]
    m_sc[...] = jnp.full_like(m_sc, -jnp.inf)
    l_sc[...] = jnp.zeros_like(l_sc)
    acc_sc[...] = jnp.zeros_like(acc_sc)

    def tile(hh, r0, kv_chunk, n_chunks, masked):
        rows = slice(r0, tq)
        cols = slice(hh * HEAD_DIM, (hh + 1) * HEAD_DIM)
        fq = fqs[hh]
        width = n_chunks * LANES
        kv0 = pl.multiple_of(kv_chunk * LANES, width)
        s = lax.dot_general(q_ref[rows, cols], k_ref[pl.ds(kv0, width), cols],
                            (((1,), (1,)), ((), ())), preferred_element_type=F32)
        fk = fk_ref[hh, pl.ds(pl.multiple_of(kv_chunk, n_chunks), n_chunks), :] * LOG2_E
        if masked:
            r_idx = lax.broadcasted_iota(jnp.int32, (LANES, LANES), 0)
            c_idx = lax.broadcasted_iota(jnp.int32, (LANES, LANES), 1)
        chunks = []
        for c in range(n_chunks):
            ch = s[:, c * LANES:(c + 1) * LANES] - fk[c:c + 1, :]
            if masked:
                top, bot = c * LANES, (c + 1) * LANES
                parts = [jnp.where(c_idx <= r_idx, ch[top:bot], -jnp.inf)]
                if top:
                    parts.insert(0, jnp.full((top, LANES), -jnp.inf, F32))
                if bot < ch.shape[0]:
                    parts.append(ch[bot:])
                ch = jnp.concatenate(parts, axis=0)
            chunks.append(ch)
        mx = functools.reduce(jnp.maximum, chunks)
        mx = jnp.max(mx, axis=1, keepdims=True)
        fq_r = fq[rows, :]
        m_prev = m_sc[hh, rows, :]
        m_new = jnp.maximum(m_prev, mx + fq_r)
        alpha = jnp.exp2(m_prev - m_new)
        shift = m_new - fq_r
        ps = [jnp.exp2(ch - shift) for ch in chunks]
        l_sc[hh, rows, :] = alpha * l_sc[hh, rows, :] + functools.reduce(lambda a, b: a + b, ps)
        p = jnp.concatenate([x.astype(BF16) for x in ps], axis=1)
        acc_sc[hh, rows, :] = alpha * acc_sc[hh, rows, :] + jnp.dot(
            p, v_ref[pl.ds(kv0, width), cols], preferred_element_type=F32)
        m_sc[hh, rows, :] = m_new

    def below_diagonal(it, carry):
        for u in range(per_q):
            for hh in range(heads):
                tile(hh, 0, (it * per_q + u) * (tk // LANES), tk // LANES, False)
        return carry

    lax.fori_loop(0, qi, below_diagonal, 0)

    for d in range(tq // td):
        for hh in range(heads):
            tile(hh, d * td, qi * (tq // LANES) + d * (td // LANES), td // LANES, True)

    for hh in range(heads):
        l_tot = jnp.sum(l_sc[hh], axis=1, keepdims=True)
        o_ref[:, hh * HEAD_DIM:(hh + 1) * HEAD_DIM] = (acc_sc[hh] / l_tot).astype(o_ref.dtype)


def _forgetting_attention(qkv, f_col, B, *, tq=2048, tk=1024, td=512, heads_per_step=2):
    T = qkv.shape[0]
    W = qkv.shape[1] // 3
    H = W // HEAD_DIM
    S = T // B
    tq = _pick_block(S, tq)
    tk = _pick_block(tq, tk)
    td = _pick_block(tk, td)
    nq = S // tq
    hp = heads_per_step if H % heads_per_step == 0 else 1
    G = H // hp
    wide = hp * HEAD_DIM
    f_row = f_col.reshape(B, S, LANES)[:, :, :H].transpose(0, 2, 1).reshape(B, G, hp, S // LANES, LANES)
    return pl.pallas_call(
        functools.partial(_attn_kernel, tk=tk, td=td),
        grid=(B, G, nq),
        in_specs=[
            pl.BlockSpec((tq, wide), lambda b, g, qi: (b * nq + qi, g)),
            pl.BlockSpec((S, wide), lambda b, g, qi: (b, G + g)),
            pl.BlockSpec((S, wide), lambda b, g, qi: (b, 2 * G + g)),
            pl.BlockSpec((tq, LANES), lambda b, g, qi: (b * nq + qi, 0)),
            pl.BlockSpec((None, None, hp, S // LANES, LANES), lambda b, g, qi: (b, g, 0, 0, 0)),
        ],
        out_specs=pl.BlockSpec((tq, wide), lambda b, g, qi: (b * nq + qi, g)),
        out_shape=jax.ShapeDtypeStruct((T, W), BF16),
        scratch_shapes=[
            pltpu.VMEM((hp, tq, LANES), F32),
            pltpu.VMEM((hp, tq, LANES), F32),
            pltpu.VMEM((hp, tq, HEAD_DIM), F32),
        ],
        compiler_params=_params(("parallel", "parallel", "arbitrary"), 56),
        name="forgetting_attention",
    )(qkv, qkv, qkv, f_col, f_row)


def _sgu_kernel(u_ref, v_ref, gv_ref, ws_ref, bs_ref, o_ref, *, groups):
    v = _norm_rows(v_ref[...].astype(F32), gv_ref[...]).astype(BF16)
    bm = v.shape[0]
    t_idx = lax.broadcasted_iota(jnp.int32, (CHUNK, CHUNK), 0)
    s_idx = lax.broadcasted_iota(jnp.int32, (CHUNK, CHUNK), 1)
    causal = s_idx <= t_idx
    for g in range(groups):
        w = jnp.where(causal, ws_ref[g], 0.0).astype(BF16)
        bias = bs_ref[:, g:g + 1]
        cols = slice(g * SGU_DIM, (g + 1) * SGU_DIM)
        for n in range(bm // CHUNK):
            rows = slice(n * CHUNK, (n + 1) * CHUNK)
            mixed = jnp.dot(w, v[rows, cols], preferred_element_type=F32) + bias
            o_ref[rows, cols] = (u_ref[rows, cols].astype(F32) * mixed).astype(o_ref.dtype)


def _chunked_sgu(ug, g_v, w_s, b_s_t, *, bm=512):
    T = ug.shape[0]
    W = ug.shape[1] // 2
    G = W // SGU_DIM
    return pl.pallas_call(
        functools.partial(_sgu_kernel, groups=G),
        grid=(T // bm,),
        in_specs=[
            pl.BlockSpec((bm, W), lambda i: (i, 0)),
            pl.BlockSpec((bm, W), lambda i: (i, 1)),
            pl.BlockSpec((1, W), lambda i: (0, 0)),
            pl.BlockSpec((G, CHUNK, CHUNK), lambda i: (0, 0, 0)),
            pl.BlockSpec((CHUNK, G), lambda i: (0, 0)),
        ],
        out_specs=pl.BlockSpec((bm, W), lambda i: (i, 0)),
        out_shape=jax.ShapeDtypeStruct((T, W), BF16),
        compiler_params=_params(("parallel",), 32),
        name="chunked_sgu",
    )(ug, ug, g_v.reshape(1, W), w_s, b_s_t)


def kernel(x, c, w_mod, b_mod, g_mix, w_in, b_f, g_v, w_s, b_s,
           w_pa, w_pm, w_o, g_ffn, w_up, w_down, g_final):
    B, S, D = x.shape
    L = w_mod.shape[0]
    T = B * S
    W_ATTN = w_pa.shape[1]
    W_SGU = w_pm.shape[1]
    H = W_ATTN // HEAD_DIM
    c_qkv = 3 * W_ATTN
    c_ug = c_qkv + H

    w_f = jnp.pad(w_in[:, :, c_qkv:c_ug].astype(BF16), ((0, 0), (0, 0), (0, LANES - H)))
    w_in_t = jnp.swapaxes(w_in, 1, 2)
    bf_pad = jnp.pad(b_f, ((0, 0), (0, LANES - H))).reshape(L, 1, LANES)
    q_scale = jnp.concatenate([jnp.full((1, W_ATTN), HEAD_DIM ** -0.5 * LOG2_E, F32),
                               jnp.ones((1, 2 * W_ATTN), F32)], axis=1)

    mod = _modulation(c, w_mod, b_mod).reshape(L, B, 6, 1, D)
    xf = x.reshape(T, D)
    for l in range(L):
        sh1, sc1, gt1, sh2, sc2, gt2 = (mod[l, :, i] for i in range(6))

        h, f_col = _prenorm_forget(xf, g_mix[l], sc1, sh1, w_f, bf_pad, l)
        qkv = _ws_matmul([h], [w_in_t], l, 0, c_qkv, "colscale", transposed=(True,),
                         extras=(q_scale,), name="proj_qkv")
        ug = _ws_matmul([h], [w_in_t], l, c_ug, 2 * W_SGU, "act", transposed=(True,),
                        act="gelu", name="proj_sgu")
        gates = _ws_matmul([h], [w_in_t], l, c_ug + 2 * W_SGU, 2 * D, "act", transposed=(True,),
                           act="sigmoid", name="proj_gates")
        a = _forgetting_attention(qkv, f_col, B)
        m = _chunked_sgu(ug, g_v[l], w_s[l], b_s[l].T)
        y = _ws_matmul([a, m], [w_pa, w_pm], l, 0, D, "merge", extras=(gates,), name="branch_merge")
        xf = _ws_matmul([y], [w_o], l, 0, D, "residual", extras=(xf, gt1), rows_per_batch=S,
                        out_dtype=F32, name="proj_out")

        h2 = _prenorm(xf, g_ffn[l], sc2, sh2)
        hid, w_down16 = _ws_matmul([h2], [w_up], l, 0, w_up.shape[2], "act", act="relu2",
                                   side_cast=w_down, name="ffn_up")
        xf = _matmul_residual(hid, w_down16, xf, gt2, name="ffn_down")
    return _rmsnorm(xf, g_final).reshape(B, S, D)
```

```python
import functools
import math

import jax
import jax.numpy as jnp
from jax import lax
from jax.experimental import pallas as pl
from jax.experimental.pallas import tpu as pltpu

HEAD_DIM = 128
SGU_DIM = 128
CHUNK = 128
EPS = 1e-6
LANES = 128
SUBLANES = 8
VMEM_LIMIT_CAP_BYTES = 60000 * 1024
VMEM_HEADROOM_BYTES = 8 * 1024 * 1024

F32 = jnp.float32
BF16 = jnp.bfloat16


def _pick_block(dim, preferred, align=LANES):
    best = None
    for cand in range(align, min(dim, preferred) + 1, align):
        if dim % cand == 0:
            best = cand
    assert best is not None, (dim, preferred, align)
    return best


def _params(dims, vmem_mb):
    return pltpu.CompilerParams(dimension_semantics=dims,
                                vmem_limit_bytes=vmem_mb * 1024 * 1024)


def _mod_kernel(c_ref, w_ref, b_ref, o_ref):
    c = c_ref[...]
    cond = (c * jax.nn.sigmoid(c)).astype(BF16)
    w = w_ref[0].astype(BF16)
    o_ref[0] = jnp.dot(cond, w, preferred_element_type=F32) + b_ref[0]


def _modulation(c, w_mod, b_mod):
    L, D, NM = w_mod.shape
    B = c.shape[0]
    rows = SUBLANES
    c_pad = jnp.zeros((rows, D), F32).at[:B].set(c)
    bn = _pick_block(NM, 512)
    out = pl.pallas_call(
        _mod_kernel,
        grid=(L, NM // bn),
        in_specs=[
            pl.BlockSpec((rows, D), lambda l, j: (0, 0)),
            pl.BlockSpec((1, D, bn), lambda l, j: (l, 0, j)),
            pl.BlockSpec((1, 1, bn), lambda l, j: (l, 0, j)),
        ],
        out_specs=pl.BlockSpec((1, rows, bn), lambda l, j: (l, 0, j)),
        out_shape=jax.ShapeDtypeStruct((L, rows, NM), F32),
        compiler_params=_params(("parallel", "parallel"), 40),
        name="modulation",
    )(c_pad, w_mod, b_mod.reshape(L, 1, NM))
    return out[:, :B]


def _norm_rows(x, g):
    ms = jnp.mean(x * x, axis=-1, keepdims=True)
    return x * lax.rsqrt(ms + EPS) * g


NORM_ROW_GROUP = 2 * SUBLANES


def _norm_by_row_groups(x_ref, g_ref, o_ref, sc_ref=None, sh_ref=None):
    D = x_ref.shape[1]
    piece = _pick_block(D, 4 * LANES)
    for r in range(0, x_ref.shape[0], NORM_ROW_GROUP):
        rows = slice(r, r + NORM_ROW_GROUP)
        x = x_ref[rows, :]
        inv = lax.rsqrt(jnp.mean(x * x, axis=-1, keepdims=True) + EPS)
        for c in range(0, D, piece):
            cols = slice(c, c + piece)
            y = x_ref[rows, cols] * inv * g_ref[:, cols]
            if sc_ref is not None:
                y = y * (1.0 + sc_ref[0, :, cols]) + sh_ref[0, :, cols]
            o_ref[rows, cols] = y.astype(o_ref.dtype)


def _prenorm_kernel(x_ref, g_ref, sc_ref, sh_ref, o_ref):
    _norm_by_row_groups(x_ref, g_ref, o_ref, sc_ref, sh_ref)


def _rmsnorm_kernel(x_ref, g_ref, o_ref):
    _norm_by_row_groups(x_ref, g_ref, o_ref)


def _log_sigmoid(x):
    return jnp.minimum(x, 0.0) - jnp.log1p(jnp.exp(-jnp.abs(x)))


def _prenorm_forget_kernel(x_ref, g_ref, sc_ref, sh_ref, wf_ref, bf_ref,
                           h_ref, f_ref, carry_ref, *, blocks_per_seq):
    i = pl.program_id(0)
    y = _norm_rows(x_ref[...], g_ref[...])
    h = (y * (1.0 + sc_ref[0]) + sh_ref[0]).astype(BF16)
    h_ref[...] = h
    zf = jnp.dot(h, wf_ref[...], preferred_element_type=F32) + bf_ref[...]
    run = _log_sigmoid(zf)
    bm = run.shape[0]
    row = lax.broadcasted_iota(jnp.int32, run.shape, 0)
    shift = 1
    while shift < bm:
        run = run + jnp.where(row >= shift, pltpu.roll(run, shift, axis=0), 0.0)
        shift *= 2

    @pl.when(i % blocks_per_seq == 0)
    def _():
        carry_ref[...] = jnp.zeros_like(carry_ref)

    run = run + carry_ref[0:1, :]
    f_ref[...] = run
    carry_ref[...] = jnp.broadcast_to(run[bm - 1:bm, :], carry_ref.shape)


def _prenorm(x, g, sc, sh, *, bm=512):
    T, D = x.shape
    B = sc.shape[0]
    S = T // B
    return pl.pallas_call(
        _prenorm_kernel,
        grid=(T // bm,),
        in_specs=[
            pl.BlockSpec((bm, D), lambda i: (i, 0)),
            pl.BlockSpec((1, D), lambda i: (0, 0)),
            pl.BlockSpec((1, 1, D), lambda i: (i // (S // bm), 0, 0)),
            pl.BlockSpec((1, 1, D), lambda i: (i // (S // bm), 0, 0)),
        ],
        out_specs=pl.BlockSpec((bm, D), lambda i: (i, 0)),
        out_shape=jax.ShapeDtypeStruct((T, D), BF16),
        compiler_params=_params(("parallel",), 48),
        name="prenorm",
    )(x, g.reshape(1, D), sc, sh)


def _prenorm_forget(x, g, sc, sh, w_f, b_f, layer, *, bm=512):
    T, D = x.shape
    B = sc.shape[0]
    S = T // B
    kern = functools.partial(_prenorm_forget_kernel, blocks_per_seq=S // bm)
    return pl.pallas_call(
        kern,
        grid=(T // bm,),
        in_specs=[
            pl.BlockSpec((bm, D), lambda i: (i, 0)),
            pl.BlockSpec((1, D), lambda i: (0, 0)),
            pl.BlockSpec((1, 1, D), lambda i: (i // (S // bm), 0, 0)),
            pl.BlockSpec((1, 1, D), lambda i: (i // (S // bm), 0, 0)),
            pl.BlockSpec((None, D, LANES), lambda i: (layer, 0, 0)),
            pl.BlockSpec((None, 1, LANES), lambda i: (layer, 0, 0)),
        ],
        out_specs=[
            pl.BlockSpec((bm, D), lambda i: (i, 0)),
            pl.BlockSpec((bm, LANES), lambda i: (i, 0)),
        ],
        out_shape=[
            jax.ShapeDtypeStruct((T, D), BF16),
            jax.ShapeDtypeStruct((T, LANES), F32),
        ],
        scratch_shapes=[pltpu.VMEM((SUBLANES, LANES), F32)],
        compiler_params=_params(("arbitrary",), 48),
        name="prenorm_forget",
    )(x, g.reshape(1, D), sc, sh, w_f, b_f)


def _rmsnorm(x, g, *, bm=512):
    T, D = x.shape
    return pl.pallas_call(
        _rmsnorm_kernel,
        grid=(T // bm,),
        in_specs=[
            pl.BlockSpec((bm, D), lambda i: (i, 0)),
            pl.BlockSpec((1, D), lambda i: (0, 0)),
        ],
        out_specs=pl.BlockSpec((bm, D), lambda i: (i, 0)),
        out_shape=jax.ShapeDtypeStruct((T, D), x.dtype),
        compiler_params=_params(("parallel",), 56),
        name="final_rmsnorm",
    )(x, g.reshape(1, D))


def _gelu_tanh(x):
    c = math.sqrt(2.0 / math.pi)
    return 0.5 * x * (1.0 + jnp.tanh(c * (x + 0.044715 * (x * x * x))))


def _apply_act(acc, act):
    if act == "gelu":
        return _gelu_tanh(acc)
    if act == "relu2":
        return jnp.square(jnp.maximum(acc, 0.0))
    if act == "sigmoid":
        return 0.5 * jnp.tanh(0.5 * acc) + 0.5
    return acc


def _ws_kernel(*refs, n_act, transposed, mode, act, side_cast):
    acts = refs[:n_act]
    chunk_refs = refs[n_act:2 * n_act]
    n_extra = {"act": 0, "colscale": 1, "residual": 2, "merge": 2}[mode]
    pos = 2 * n_act + n_extra
    extra = refs[2 * n_act:pos]
    if side_cast:
        side_in, o_ref, side_out = refs[pos:pos + 3]
        bufs = refs[pos + 3:]
    else:
        o_ref = refs[pos]
        bufs = refs[pos + 1:]
    j = pl.program_id(0)
    i = pl.program_id(1)

    def stage(parity):
        if side_cast:
            side_out[...] = side_in[...].astype(BF16)
        for t in range(n_act):
            buf = bufs[2 * t + parity]
            if transposed[t]:
                chunk = chunk_refs[t][0].astype(BF16)
                ck = chunk.shape[1]
                buf[:, pl.ds(pl.multiple_of(i * ck, ck), ck)] = chunk
            else:
                chunk = chunk_refs[t][...].astype(BF16)
                ck = chunk.shape[0]
                buf[pl.ds(pl.multiple_of(i * ck, ck), ck), :] = chunk

    def multiply(parity):
        accs = []
        for t in range(n_act):
            w = bufs[2 * t + parity][...]
            dims = (((1,), (1,)), ((), ())) if transposed[t] else (((1,), (0,)), ((), ()))
            accs.append(lax.dot_general(acts[t][...], w, dims, preferred_element_type=F32))
        if mode == "act":
            y = _apply_act(accs[0], act)
        elif mode == "colscale":
            y = accs[0] * extra[0][...]
        elif mode == "residual":
            y = extra[0][...] + extra[1][0] * accs[0]
        else:
            y = extra[0][...].astype(F32) * accs[0] + extra[1][...].astype(F32) * accs[1]
        o_ref[...] = y.astype(o_ref.dtype)

    @pl.when(j == 0)
    def _():
        stage(0)

    for parity in (0, 1):
        @pl.when(jnp.logical_and(j > 0, j % 2 == parity))
        def _():
            stage(parity)
            multiply(1 - parity)


def _ws_matmul(acts, weights, layer, src_col0, n_cols, mode, *, transposed=None, act="none",
               extras=(), rows_per_batch=None, out_dtype=BF16, side_cast=None, bm=1024, bn=1024,
               name):
    n_act = len(acts)
    transposed = tuple(transposed or (False,) * n_act)
    M = acts[0].shape[0]
    bm = _pick_block(M if rows_per_batch is None else rows_per_batch, bm)
    bn = _pick_block(n_cols, bn)
    ni, nj = M // bm, n_cols // bn
    last = nj - 1

    def row_blk(j, i):
        return jnp.where(j == 0, 0, i)

    in_specs, args = [], []
    for a in acts:
        in_specs.append(pl.BlockSpec((bm, a.shape[1]), lambda j, i: (row_blk(j, i), 0)))
        args.append(a)
    scratch = []
    for a, w, tr in zip(acts, weights, transposed):
        K = a.shape[1]
        assert K % ni == 0, (K, ni)
        ck = K // ni
        if tr:
            assert ck % LANES == 0 and src_col0 % SUBLANES == 0, (ck, src_col0)
            in_specs.append(pl.BlockSpec(
                (pl.Element(1), pl.Element(bn), pl.Element(ck)),
                lambda j, i: (layer,
                              pl.multiple_of(src_col0 + jnp.minimum(j, last) * bn, SUBLANES),
                              pl.multiple_of(i * ck, LANES))))
            scratch += [pltpu.VMEM((bn, K), BF16)] * 2
        else:
            assert ck % (2 * SUBLANES) == 0 and src_col0 % bn == 0, (ck, src_col0, bn)
            in_specs.append(pl.BlockSpec(
                (None, ck, bn), lambda j, i: (layer, i, src_col0 // bn + jnp.minimum(j, last))))
            scratch += [pltpu.VMEM((K, bn), BF16)] * 2
        args.append(w)

    def out_blk(j, i):
        return (row_blk(j, i), jnp.maximum(j - 1, 0))

    if mode == "colscale":
        in_specs.append(pl.BlockSpec((1, bn), lambda j, i: (0, jnp.maximum(j - 1, 0))))
    elif mode == "residual":
        in_specs.append(pl.BlockSpec((bm, bn), out_blk))
        per_batch = rows_per_batch // bm
        in_specs.append(pl.BlockSpec(
            (1, 1, bn), lambda j, i: (row_blk(j, i) // per_batch, 0, jnp.maximum(j - 1, 0))))
    elif mode == "merge":
        in_specs.append(pl.BlockSpec((bm, bn), out_blk))
        in_specs.append(pl.BlockSpec(
            (bm, bn), lambda j, i: (row_blk(j, i), jnp.maximum(j - 1, 0) + nj)))
        extras = (extras[0], extras[0])
    args.extend(extras)

    out_specs = pl.BlockSpec((bm, bn), out_blk)
    out_shape = jax.ShapeDtypeStruct((M, n_cols), out_dtype)
    side_bytes = 0
    if side_cast is not None:
        _, R, C = side_cast.shape
        assert R % (nj * ni) == 0 and (R // (nj * ni)) % (2 * SUBLANES) == 0, (R, nj, ni)
        rs = R // (nj * ni)

        def slab(j, i):
            return jnp.maximum(j - 1, 0) * ni + row_blk(j, i)

        in_specs.append(pl.BlockSpec((None, rs, C), lambda j, i: (layer, slab(j, i), 0)))
        args.append(side_cast)
        out_specs = [out_specs, pl.BlockSpec((rs, C), lambda j, i: (slab(j, i), 0))]
        out_shape = [out_shape, jax.ShapeDtypeStruct((R, C), BF16)]
        side_bytes = 2 * rs * C * (4 + 2)

    out_bytes = jnp.dtype(out_dtype).itemsize
    vmem = sum(2 * bm * a.shape[1] * 2 + 2 * a.shape[1] * bn * 2 + 2 * (a.shape[1] // ni) * bn * 4
               for a in acts)
    vmem += 2 * bm * bn * out_bytes + n_act * bm * bn * 4
    vmem += {"act": 0, "colscale": 0, "residual": 2 * bm * bn * 4, "merge": 4 * bm * bn * 2}[mode]
    vmem += side_bytes
    kern = functools.partial(_ws_kernel, n_act=n_act, transposed=transposed, mode=mode, act=act,
                             side_cast=side_cast is not None)
    return pl.pallas_call(
        kern,
        grid=(nj + 1, ni),
        in_specs=in_specs,
        out_specs=out_specs,
        out_shape=out_shape,
        scratch_shapes=scratch,
        compiler_params=pltpu.CompilerParams(
            dimension_semantics=("arbitrary", "arbitrary"),
            vmem_limit_bytes=min(vmem + VMEM_HEADROOM_BYTES, VMEM_LIMIT_CAP_BYTES)),
        name=name,
    )(*args)


def _mm_res_kernel(x_ref, w_ref, r_ref, gt_ref, o_ref):
    def gated_product():
        return gt_ref[0] * jnp.dot(x_ref[...], w_ref[...], preferred_element_type=F32)

    @pl.when(pl.program_id(2) == 0)
    def _():
        o_ref[...] = r_ref[...] + gated_product()

    @pl.when(pl.program_id(2) > 0)
    def _():
        o_ref[...] += gated_product()


def _matmul_residual(x, w, res, gate, *, bm=1024, bn=1024, bk=4096, name):
    M, K = x.shape
    N = w.shape[1]
    B = gate.shape[0]
    S = M // B
    bm = _pick_block(S, bm)
    bn = _pick_block(N, bn)
    bk = _pick_block(K, bk)
    return pl.pallas_call(
        _mm_res_kernel,
        grid=(M // bm, N // bn, K // bk),
        in_specs=[
            pl.BlockSpec((bm, bk), lambda i, j, k: (i, k)),
            pl.BlockSpec((bk, bn), lambda i, j, k: (k, j)),
            pl.BlockSpec((bm, bn), lambda i, j, k: (i, j)),
            pl.BlockSpec((1, 1, bn), lambda i, j, k: (i // (S // bm), 0, j)),
        ],
        out_specs=pl.BlockSpec((bm, bn), lambda i, j, k: (i, j)),
        out_shape=jax.ShapeDtypeStruct((M, N), F32),
        compiler_params=_params(("parallel", "parallel", "arbitrary"), 56),
        name=name,
    )(x, w, res, gate)


LOG2_E = math.log2(math.e)


def _attn_kernel(q_ref, k_ref, v_ref, fq_ref, fk_ref, o_ref, m_sc, l_sc, acc_sc, *, tk, td):
    hg = pl.program_id(1)
    qi = pl.program_id(2)
    tq = q_ref.shape[0]
    heads = q_ref.shape[1] // HEAD_DIM
    per_q = tq // tk

    f_all = fq_ref[...]
    lane = lax.broadcasted_iota(jnp.int32, f_all.shape, 1)
    fqs = [jnp.broadcast_to(
        jnp.sum(jnp.where(lane == hg * heads + hh, f_all, 0.0), axis=1, keepdims=True) * LOG2_E,
        (tq, LANES)) for hh in range(heads)]
    m_sc[...] = jnp.full_like(m_sc, -jnp.inf)
    l_sc[...] = jnp.zeros_like(l_sc)
    acc_sc[...] = jnp.zeros_like(acc_sc)

    def tile(hh, r0, kv_chunk, n_chunks, masked):
        rows = slice(r0, tq)
        cols = slice(hh * HEAD_DIM, (hh + 1) * HEAD_DIM)
        fq = fqs[hh]
        width = n_chunks * LANES
        kv0 = pl.multiple_of(kv_chunk * LANES, width)
        s = lax.dot_general(q_ref[rows, cols], k_ref[pl.ds(kv0, width), cols],
                            (((1,), (1,)), ((), ())), preferred_element_type=F32)
        fk = fk_ref[hh, pl.ds(pl.multiple_of(kv_chunk, n_chunks), n_chunks), :] * LOG2_E
        if masked:
            r_idx = lax.broadcasted_iota(jnp.int32, (LANES, LANES), 0)
            c_idx = lax.broadcasted_iota(jnp.int32, (LANES, LANES), 1)
        chunks = []
        for c in range(n_chunks):
            ch = s[:, c * LANES:(c + 1) * LANES] - fk[c:c + 1, :]
            if masked:
                top, bot = c * LANES, (c + 1) * LANES
                parts = [jnp.where(c_idx <= r_idx, ch[top:bot], -jnp.inf)]
                if top:
                    parts.insert(0, jnp.full((top, LANES), -jnp.inf, F32))
                if bot < ch.shape[0]:
                    parts.append(ch[bot:])
                ch = jnp.concatenate(parts, axis=0)
            chunks.append(ch)
        mx = functools.reduce(jnp.maximum, chunks)
        mx = jnp.max(mx, axis=1, keepdims=True)
        fq_r = fq[rows, :]
        m_prev = m_sc[hh, rows, :]
        m_new = jnp.maximum(m_prev, mx + fq_r)
        alpha = jnp.exp2(m_prev - m_new)
        shift = m_new - fq_r
        ps = [jnp.exp2(ch - shift) for ch in chunks]
        l_sc[hh, rows, :] = alpha * l_sc[hh, rows, :] + functools.reduce(lambda a, b: a + b, ps)
        p = jnp.concatenate([x.astype(BF16) for x in ps], axis=1)
        acc_sc[hh, rows, :] = alpha * acc_sc[hh, rows, :] + jnp.dot(
            p, v_ref[pl.ds(kv0, width), cols], preferred_element_type=F32)
        m_sc[hh, rows, :] = m_new

    def below_diagonal(it, carry):
        for u in range(per_q):
            for hh in range(heads):
                tile(hh, 0, (it * per_q + u) * (tk // LANES), tk // LANES, False)
        return carry

    lax.fori_loop(0, qi, below_diagonal, 0)

    for d in range(tq // td):
        for hh in range(heads):
            tile(hh, d * td, qi * (tq // LANES) + d * (td // LANES), td // LANES, True)

    for hh in range(heads):
        l_tot = jnp.sum(l_sc[hh], axis=1, keepdims=True)
        o_ref[:, hh * HEAD_DIM:(hh + 1) * HEAD_DIM] = (acc_sc[hh] / l_tot).astype(o_ref.dtype)


def _forgetting_attention(qkv, f_col, B, *, tq=2048, tk=1024, td=512, heads_per_step=2):
    T = qkv.shape[0]
    W = qkv.shape[1] // 3
    H = W // HEAD_DIM
    S = T // B
    tq = _pick_block(S, tq)
    tk = _pick_block(tq, tk)
    td = _pick_block(tk, td)
    nq = S // tq
    hp = heads_per_step if H % heads_per_step == 0 else 1
    G = H // hp
    wide = hp * HEAD_DIM
    f_row = f_col.reshape(B, S, LANES)[:, :, :H].transpose(0, 2, 1).reshape(B, G, hp, S // LANES, LANES)
    return pl.pallas_call(
        functools.partial(_attn_kernel, tk=tk, td=td),
        grid=(B, G, nq),
        in_specs=[
            pl.BlockSpec((tq, wide), lambda b, g, qi: (b * nq + qi, g)),
            pl.BlockSpec((S, wide), lambda b, g, qi: (b, G + g)),
            pl.BlockSpec((S, wide), lambda b, g, qi: (b, 2 * G + g)),
            pl.BlockSpec((tq, LANES), lambda b, g, qi: (b * nq + qi, 0)),
            pl.BlockSpec((None, None, hp, S // LANES, LANES), lambda b, g, qi: (b, g, 0, 0, 0)),
        ],
        out_specs=pl.BlockSpec((tq, wide), lambda b, g, qi: (b * nq + qi, g)),
        out_shape=jax.ShapeDtypeStruct((T, W), BF16),
        scratch_shapes=[
            pltpu.VMEM((hp, tq, LANES), F32),
            pltpu.VMEM((hp, tq, LANES), F32),
            pltpu.VMEM((hp, tq, HEAD_DIM), F32),
        ],
        compiler_params=_params(("parallel", "parallel", "arbitrary"), 56),
        name="forgetting_attention",
    )(qkv, qkv, qkv, f_col, f_row)


def _sgu_kernel(u_ref, v_ref, gv_ref, ws_ref, bs_ref, o_ref, *, groups):
    v = _norm_rows(v_ref[...].astype(F32), gv_ref[...]).astype(BF16)
    bm = v.shape[0]
    t_idx = lax.broadcasted_iota(jnp.int32, (CHUNK, CHUNK), 0)
    s_idx = lax.broadcasted_iota(jnp.int32, (CHUNK, CHUNK), 1)
    causal = s_idx <= t_idx
    for g in range(groups):
        w = jnp.where(causal, ws_ref[g], 0.0).astype(BF16)
        bias = bs_ref[:, g:g + 1]
        cols = slice(g * SGU_DIM, (g + 1) * SGU_DIM)
        for n in range(bm // CHUNK):
            rows = slice(n * CHUNK, (n + 1) * CHUNK)
            mixed = jnp.dot(w, v[rows, cols], preferred_element_type=F32) + bias
            o_ref[rows, cols] = (u_ref[rows, cols].astype(F32) * mixed).astype(o_ref.dtype)


def _chunked_sgu(ug, g_v, w_s, b_s_t, *, bm=512):
    T = ug.shape[0]
    W = ug.shape[1] // 2
    G = W // SGU_DIM
    return pl.pallas_call(
        functools.partial(_sgu_kernel, groups=G),
        grid=(T // bm,),
        in_specs=[
            pl.BlockSpec((bm, W), lambda i: (i, 0)),
            pl.BlockSpec((bm, W), lambda i: (i, 1)),
            pl.BlockSpec((1, W), lambda i: (0, 0)),
            pl.BlockSpec((G, CHUNK, CHUNK), lambda i: (0, 0, 0)),
            pl.BlockSpec((CHUNK, G), lambda i: (0, 0)),
        ],
        out_specs=pl.BlockSpec((bm, W), lambda i: (i, 0)),
        out_shape=jax.ShapeDtypeStruct((T, W), BF16),
        compiler_params=_params(("parallel",), 32),
        name="chunked_sgu",
    )(ug, ug, g_v.reshape(1, W), w_s, b_s_t)


def kernel(x, c, w_mod, b_mod, g_mix, w_in, b_f, g_v, w_s, b_s,
           w_pa, w_pm, w_o, g_ffn, w_up, w_down, g_final):
    B, S, D = x.shape
    L = w_mod.shape[0]
    T = B * S
    W_ATTN = w_pa.shape[1]
    W_SGU = w_pm.shape[1]
    H = W_ATTN // HEAD_DIM
    c_qkv = 3 * W_ATTN
    c_ug = c_qkv + H

    w_f = jnp.pad(w_in[:, :, c_qkv:c_ug].astype(BF16), ((0, 0), (0, 0), (0, LANES - H)))
    w_in_t = jnp.swapaxes(w_in, 1, 2)
    bf_pad = jnp.pad(b_f, ((0, 0), (0, LANES - H))).reshape(L, 1, LANES)
    q_scale = jnp.concatenate([jnp.full((1, W_ATTN), HEAD_DIM ** -0.5 * LOG2_E, F32),
                               jnp.ones((1, 2 * W_ATTN), F32)], axis=1)

    mod = _modulation(c, w_mod, b_mod).reshape(L, B, 6, 1, D)
    xf = x.reshape(T, D)
    for l in range(L):
        sh1, sc1, gt1, sh2, sc2, gt2 = (mod[l, :, i] for i in range(6))

        h, f_col = _prenorm_forget(xf, g_mix[l], sc1, sh1, w_f, bf_pad, l)
        qkv = _ws_matmul([h], [w_in_t], l, 0, c_qkv, "colscale", transposed=(True,),
                         extras=(q_scale,), name="proj_qkv")
        ug = _ws_matmul([h], [w_in_t], l, c_ug, 2 * W_SGU, "act", transposed=(True,),
                        act="gelu", name="proj_sgu")
        gates = _ws_matmul([h], [w_in_t], l, c_ug + 2 * W_SGU, 2 * D, "act", transposed=(True,),
                           act="sigmoid", name="proj_gates")
        a = _forgetting_attention(qkv, f_col, B)
        m = _chunked_sgu(ug, g_v[l], w_s[l], b_s[l].T)
        y = _ws_matmul([a, m], [w_pa, w_pm], l, 0, D, "merge", extras=(gates,), name="branch_merge")
        xf = _ws_matmul([y], [w_o], l, 0, D, "residual", extras=(xf, gt1), rows_per_batch=S,
                        out_dtype=F32, name="proj_out")

        h2 = _prenorm(xf, g_ffn[l], sc2, sh2)
        hid, w_down16 = _ws_matmul([h2], [w_up], l, 0, w_up.shape[2], "act", act="relu2",
                                   side_cast=w_down, name="ffn_up")
        xf = _matmul_residual(hid, w_down16, xf, gt2, name="ffn_down")
    return _rmsnorm(xf, g_final).reshape(B, S, D)
```
